```python
import jax
import jax.numpy as jnp
from jax import lax
import numpy as np

D_MODEL = 1024
BATCH = 32
SEQ = 2048
DEPTH = 1

CHUNK = 64
HEAD_DIM = 64
N_HEADS_A = 8
LEFT_CHUNKS_A = 8
MAX_REL = 128
N_HEADS_B = 8
N_KV_B = 2
WINDOW_B = 128
LEFT_CHUNKS_B = WINDOW_B // CHUNK
ROT_DIM = HEAD_DIM // 4
ROPE_THETA = 500000.0
D_FF = 2816
EPS = 1e-6
NEG_INF = -1e30

W_A = N_HEADS_A * HEAD_DIM
W_QB = N_HEADS_B * HEAD_DIM
W_KVB = N_KV_B * HEAD_DIM
D_IN = 3 * W_A + W_QB + 2 * W_KVB
D_MIX = W_A + W_QB

kernel_name = "hybrid_chunk_stream_encoder_layer"


def rms_norm(x, g):
    xf = x.astype(jnp.float32)
    y = xf * lax.rsqrt(jnp.mean(xf * xf, axis=-1, keepdims=True) + EPS)
    return (y * g.astype(jnp.float32)).astype(x.dtype)


def swiglu(x, w_gate, w_up, w_down):
    return (jax.nn.silu(x @ w_gate) * (x @ w_up)) @ w_down


def partial_rope(x, pos):
    half = ROT_DIM // 2
    inv_freq = jnp.power(jnp.float32(ROPE_THETA), -jnp.arange(half, dtype=jnp.float32) * (2.0 / ROT_DIM))
    ang = pos.astype(jnp.float32)[..., None] * inv_freq
    cos = jnp.cos(ang)[:, :, None, :]
    sin = jnp.sin(ang)[:, :, None, :]
    xr = x[..., :ROT_DIM].astype(jnp.float32)
    x1, x2 = xr[..., :half], xr[..., half:]
    rot = jnp.concatenate([x1 * cos - x2 * sin, x2 * cos + x1 * sin], axis=-1).astype(x.dtype)
    return jnp.concatenate([rot, x[..., ROT_DIM:]], axis=-1)


def rel_position_bias(table):
    band = (LEFT_CHUNKS_A + 1) * CHUNK
    i = jnp.arange(CHUNK)[:, None]
    j = jnp.arange(band)[None, :]
    dist = LEFT_CHUNKS_A * CHUNK + i - j
    idx = jnp.clip(dist, -MAX_REL, MAX_REL) + MAX_REL
    return table[:, idx]


def band_attention(q, k, v, left_chunks, bias, sinks):
    b, s, hkv, g, dh = q.shape
    n_chunks = s // CHUNK
    band = (left_chunks + 1) * CHUNK
    pad = left_chunks * CHUNK
    kp = jnp.pad(k, ((0, 0), (pad, 0), (0, 0), (0, 0)))
    vp = jnp.pad(v, ((0, 0), (pad, 0), (0, 0), (0, 0)))
    scale = dh ** -0.5
    key_off = jnp.arange(band) - pad

    def one_chunk(c):
        start = c * CHUNK
        qc = lax.dynamic_slice_in_dim(q, start, CHUNK, axis=1).astype(jnp.float32)
        kc = lax.dynamic_slice_in_dim(kp, start, band, axis=1).astype(jnp.float32)
        vc = lax.dynamic_slice_in_dim(vp, start, band, axis=1).astype(jnp.float32)
        logits = jnp.einsum('bqhgd,bkhd->bhgqk', qc, kc) * scale
        if bias is not None:
            logits = logits + bias.astype(jnp.float32)
        valid = (start + key_off) >= 0
        logits = jnp.where(valid, logits, NEG_INF)
        if sinks is None:
            p = jax.nn.softmax(logits, axis=-1)
        else:
            sk = sinks.astype(jnp.float32)[None, :, :, None, None]
            m = jnp.maximum(jnp.max(logits, axis=-1, keepdims=True), sk)
            e = jnp.exp(logits - m)
            p = e / (jnp.sum(e, axis=-1, keepdims=True) + jnp.exp(sk - m))
        out = jnp.einsum('bhgqk,bkhd->bqhgd', p, vc)
        return out.astype(q.dtype)

    outs = lax.map(one_chunk, jnp.arange(n_chunks))
    return jnp.moveaxis(outs, 0, 1).reshape(b, s, hkv * g, dh)


def hybrid_mixer(u, positions, w_in, rel_table_a, sinks_b, w_out):
    b, s, _ = u.shape
    proj = u @ w_in
    qa, ka, va, qb, kb, vb = jnp.split(
        proj, [W_A, 2 * W_A, 3 * W_A, 3 * W_A + W_QB, 3 * W_A + W_QB + W_KVB], axis=-1)
    qa = qa.reshape(b, s, N_HEADS_A, 1, HEAD_DIM)
    ka = ka.reshape(b, s, N_HEADS_A, HEAD_DIM)
    va = va.reshape(b, s, N_HEADS_A, HEAD_DIM)
    bias_a = rel_position_bias(rel_table_a)[:, None]
    oa = band_attention(qa, ka, va, LEFT_CHUNKS_A, bias_a, None).reshape(b, s, W_A)
    qb = partial_rope(qb.reshape(b, s, N_HEADS_B, HEAD_DIM), positions)
    qb = qb.reshape(b, s, N_KV_B, N_HEADS_B // N_KV_B, HEAD_DIM)
    kb = partial_rope(kb.reshape(b, s, N_KV_B, HEAD_DIM), positions)
    vb = vb.reshape(b, s, N_KV_B, HEAD_DIM)
    sk = sinks_b.reshape(N_KV_B, N_HEADS_B // N_KV_B)
    ob = band_attention(qb, kb, vb, LEFT_CHUNKS_B, None, sk).reshape(b, s, W_QB)
    return jnp.concatenate([oa, ob], axis=-1) @ w_out


def setup_inputs(seed: int = 0) -> dict:
    key = jax.random.key(seed)
    ks = jax.random.split(key, 20)
    f32 = jnp.float32

    def nrm(k, shape, scale):
        return jax.random.normal(k, shape, f32) * scale

    def gain(k):
        return 1.0 + 0.05 * jax.random.normal(k, (DEPTH, D_MODEL), f32)

    x = jax.random.normal(ks[0], (BATCH, SEQ, D_MODEL), f32)
    offset = jax.random.randint(ks[1], (BATCH, 1), 0, 256) * CHUNK
    positions = (offset + jnp.arange(SEQ, dtype=jnp.int32)[None, :]).astype(jnp.int32)
    return {
        "x": x,
        "positions": positions,
        "ffn1_pre_g": gain(ks[2]),
        "ffn1_w_gate": nrm(ks[3], (DEPTH, D_MODEL, D_FF), D_MODEL ** -0.5),
        "ffn1_w_up": nrm(ks[4], (DEPTH, D_MODEL, D_FF), D_MODEL ** -0.5),
        "ffn1_w_down": nrm(ks[5], (DEPTH, D_FF, D_MODEL), D_FF ** -0.5),
        "ffn1_post_g": gain(ks[6]),
        "mix_pre_g": gain(ks[7]),
        "w_in": nrm(ks[8], (DEPTH, D_MODEL, D_IN), D_MODEL ** -0.5),
        "rel_bias_a": nrm(ks[9], (DEPTH, N_HEADS_A, 2 * MAX_REL + 1), 0.1),
        "sinks_b": nrm(ks[10], (DEPTH, N_HEADS_B), 0.5),
        "w_out": nrm(ks[11], (DEPTH, D_MIX, D_MODEL), D_MIX ** -0.5),
        "mix_post_g": gain(ks[12]),
        "ffn2_pre_g": gain(ks[13]),
        "ffn2_w_gate": nrm(ks[14], (DEPTH, D_MODEL, D_FF), D_MODEL ** -0.5),
        "ffn2_w_up": nrm(ks[15], (DEPTH, D_MODEL, D_FF), D_MODEL ** -0.5),
        "ffn2_w_down": nrm(ks[16], (DEPTH, D_FF, D_MODEL), D_FF ** -0.5),
        "ffn2_post_g": gain(ks[17]),
        "final_g": gain(ks[18]),
    }


def reference(x, positions, ffn1_pre_g, ffn1_w_gate, ffn1_w_up, ffn1_w_down, ffn1_post_g,
              mix_pre_g, w_in, rel_bias_a, sinks_b, w_out, mix_post_g,
              ffn2_pre_g, ffn2_w_gate, ffn2_w_up, ffn2_w_down, ffn2_post_g, final_g):
    h = x
    for l in range(DEPTH):
        f = swiglu(rms_norm(h, ffn1_pre_g[l]), ffn1_w_gate[l], ffn1_w_up[l], ffn1_w_down[l])
        h = h + 0.5 * rms_norm(f, ffn1_post_g[l])
        m = hybrid_mixer(rms_norm(h, mix_pre_g[l]), positions, w_in[l], rel_bias_a[l],
                         sinks_b[l], w_out[l])
        h = h + rms_norm(m, mix_post_g[l])
        f = swiglu(rms_norm(h, ffn2_pre_g[l]), ffn2_w_gate[l], ffn2_w_up[l], ffn2_w_down[l])
        h = h + 0.5 * rms_norm(f, ffn2_post_g[l])
        h = rms_norm(h, final_g[l])
    return h
```

```python
import functools

import jax
import jax.numpy as jnp
from jax import lax
from jax.experimental import pallas as pl
from jax.experimental.pallas import tpu as pltpu

D_MODEL = 1024
SEQ = 2048
CHUNK = 64
HEAD_DIM = 64
N_HEADS_A = 8
LEFT_CHUNKS_A = 8
MAX_REL = 128
N_HEADS_B = 8
N_KV_B = 2
LEFT_CHUNKS_B = 2
ROT_DIM = HEAD_DIM // 4
ROPE_THETA = 500000.0
D_FF = 2816
EPS = 1e-6
NEG_INF = -1e30

W_A = N_HEADS_A * HEAD_DIM
W_QB = N_HEADS_B * HEAD_DIM
W_KVB = N_KV_B * HEAD_DIM
W_KVB_DUP = 2 * W_KVB
D_IN_EXT = 3 * W_A + W_QB + 2 * W_KVB_DUP
BAND_A = (LEFT_CHUNKS_A + 1) * CHUNK
BAND_B = (LEFT_CHUNKS_B + 1) * CHUNK
PAD_A = LEFT_CHUNKS_A * CHUNK
PAD_B = LEFT_CHUNKS_B * CHUNK
SCALE = HEAD_DIM ** -0.5

LANES = 128
TOEPLITZ_W = 768
REL_PAD = 384

TOKENS_PER_STEP = 512
FF_CHUNKS = ((0, 1024), (1024, 1024), (2048, 768))
VMEM_LIMIT_BYTES = 60 * 1024 * 1024

_F32 = jnp.float32
_BF16 = jnp.bfloat16
_NT = (((1,), (1,)), ((), ()))


def _rms_norm(x, g):
    return x * lax.rsqrt(jnp.mean(x * x, axis=-1, keepdims=True) + EPS) * g


def _swiglu(xn, wg_ref, wu_ref, wd_ref):
    acc = None
    for lo, size in FF_CHUNKS:
        g = jnp.dot(xn, wg_ref[:, lo:lo + size], preferred_element_type=_F32)
        u = jnp.dot(xn, wu_ref[:, lo:lo + size], preferred_element_type=_F32)
        a = (g * (1.0 / (1.0 + jnp.exp(-g))) * u).astype(_BF16)
        part = jnp.dot(a, wd_ref[lo:lo + size, :], preferred_element_type=_F32)
        acc = part if acc is None else acc + part
    return acc


def _rope_cols(x, cos, sin_lo, sin_hi):
    return x * cos + pltpu.roll(x, ROT_DIM // 2, 1) * sin_hi + pltpu.roll(x, LANES - ROT_DIM // 2, 1) * sin_lo


def _ffn_proj_kernel(x_ref, pos_ref, freq_ref, pre1_ref, wg_ref, wu_ref, wd_ref, post1_ref, prem_ref, win_ref,
                     h1_ref, qa_ref, ka_ref, va_ref, qb_ref, kb_ref, vb_ref):
    x = x_ref[0]
    f = _swiglu(_rms_norm(x, pre1_ref[...]).astype(_BF16), wg_ref, wu_ref, wd_ref)
    h1 = x + 0.5 * _rms_norm(f, post1_ref[...])
    h1_ref[0] = h1

    u = _rms_norm(h1, prem_ref[...]).astype(_BF16)
    proj = jnp.dot(u, win_ref[...], preferred_element_type=_F32)
    qa_ref[0] = (proj[:, 0:W_A] * SCALE).astype(_BF16)
    ka_ref[0] = proj[:, W_A:2 * W_A].astype(_BF16)
    va_ref[0] = proj[:, 2 * W_A:3 * W_A].astype(_BF16)

    ang = pos_ref[0].astype(_F32) * freq_ref[...]
    cos = jnp.cos(ang)
    sin = jnp.sin(ang)
    lane = lax.broadcasted_iota(jnp.int32, ang.shape, 1) % HEAD_DIM
    sin_lo = jnp.where(lane < ROT_DIM // 2, -sin, 0.0)
    sin_hi = jnp.where(lane >= ROT_DIM // 2, sin, 0.0)
    qb0 = 3 * W_A
    for j in range(W_QB // LANES):
        blk = proj[:, qb0 + j * LANES:qb0 + (j + 1) * LANES]
        qb_ref[0, :, j * LANES:(j + 1) * LANES] = (_rope_cols(blk, cos, sin_lo, sin_hi) * SCALE).astype(_BF16)
    kb0 = qb0 + W_QB
    for j in range(W_KVB_DUP // LANES):
        blk = proj[:, kb0 + j * LANES:kb0 + (j + 1) * LANES]
        kb_ref[0, :, j * LANES:(j + 1) * LANES] = _rope_cols(blk, cos, sin_lo, sin_hi).astype(_BF16)
    vb_ref[0] = proj[:, kb0 + W_KVB_DUP:kb0 + 2 * W_KVB_DUP].astype(_BF16)


def _build_bias(rel_ref, bias_ref):
    m = lax.broadcasted_iota(jnp.int32, (REL_PAD, TOEPLITZ_W), 1)
    r = lax.broadcasted_iota(jnp.int32, (REL_PAD, TOEPLITZ_W), 0)
    sel = (r == jnp.clip(PAD_A + CHUNK - 1 - m, -MAX_REL, MAX_REL) + MAX_REL).astype(_F32)
    g = jnp.dot(rel_ref[...], sel, preferred_element_type=_F32, precision=lax.Precision.HIGHEST)
    for h in range(N_HEADS_A):
        rows = jnp.broadcast_to(g[h:h + 1, :], (CHUNK, TOEPLITZ_W))
        rolled = pltpu.roll(rows, TOEPLITZ_W - (CHUNK - 1), 1, stride=1, stride_axis=0)
        bias_ref[h // 4, (h % 4) * CHUNK:(h % 4 + 1) * CHUNK, :] = rolled[:, :BAND_A]


def _mixer_ffn_kernel(sinks_ref, h1_ref, qa_ref, qb_ref, ka_ref, va_ref, kb_ref, vb_ref,
                      kap_ref, vap_ref, kbp_ref, vbp_ref, rel_ref,
                      wout_ref, postm_ref, pre2_ref, wg_ref, wu_ref, wd_ref, post2_ref, final_ref,
                      out_ref, ka_all, va_all, kb_all, vb_all, o_scr, bias_scr):
    step = pl.program_id(1)
    n_tok = TOKENS_PER_STEP

    @pl.when((pl.program_id(0) == 0) & (step == 0))
    def _():
        _build_bias(rel_ref, bias_scr)

    ka_all[0:PAD_A] = kap_ref[0]
    va_all[0:PAD_A] = vap_ref[0]
    ka_all[PAD_A:PAD_A + n_tok] = ka_ref[0]
    va_all[PAD_A:PAD_A + n_tok] = va_ref[0]
    kb_all[0:PAD_B] = kbp_ref[0]
    vb_all[0:PAD_B] = vbp_ref[0]
    kb_all[PAD_B:PAD_B + n_tok] = kb_ref[0]
    vb_all[PAD_B:PAD_B + n_tok] = vb_ref[0]

    row_head = lax.broadcasted_iota(jnp.int32, (4 * CHUNK, 4 * HEAD_DIM), 0) // CHUNK
    lane_head = lax.broadcasted_iota(jnp.int32, (4 * CHUNK, 4 * HEAD_DIM), 1) // HEAD_DIM
    block_diag = row_head == lane_head
    lane_head_a = lax.broadcasted_iota(jnp.int32, (CHUNK, 4 * HEAD_DIM), 1) // HEAD_DIM
    col_a = lax.broadcasted_iota(jnp.int32, (4 * CHUNK, BAND_A), 1)
    col_b = lax.broadcasted_iota(jnp.int32, (4 * CHUNK, BAND_B), 1)
    low_half = lax.broadcasted_iota(jnp.int32, (CHUNK, LANES), 1) < HEAD_DIM
    row_blk = lax.broadcasted_iota(jnp.int32, (4 * CHUNK, 1), 0) // CHUNK
    sink_cols = []
    for kv in range(N_KV_B):
        col = jnp.zeros((4 * CHUNK, 1), _F32)
        for gq in range(4):
            col = jnp.where(row_blk == gq, sinks_ref[4 * kv + gq], col)
        sink_cols.append(col)

    def chunk_body(c, carry):
        r0 = pl.multiple_of(c * CHUNK, CHUNK)
        inv_a = jnp.where(step == 0, PAD_A - c * CHUNK, 0)
        inv_b = jnp.where(step == 0, PAD_B - c * CHUNK, 0)

        qa = qa_ref[0, pl.ds(r0, CHUNK), :]
        for g in range(2):
            cols = slice(g * 4 * HEAD_DIM, (g + 1) * 4 * HEAD_DIM)
            q4 = qa[:, cols]
            q_bd = jnp.where(block_diag, jnp.concatenate([q4] * 4, axis=0), jnp.zeros((), _BF16))
            k_band = ka_all[pl.ds(r0, BAND_A), cols]
            v_band = va_all[pl.ds(r0, BAND_A), cols]
            s = lax.dot_general(q_bd, k_band, _NT, preferred_element_type=_F32) + bias_scr[g]
            s = jnp.where(col_a < inv_a, NEG_INF, s)
            p = jnp.exp(s - jnp.max(s, axis=-1, keepdims=True))
            denom = jnp.sum(p, axis=-1, keepdims=True)
            o_full = jnp.dot(p.astype(_BF16), v_band, preferred_element_type=_F32) / denom
            o = jnp.zeros((CHUNK, 4 * HEAD_DIM), _F32)
            for h in range(4):
                o = jnp.where(lane_head_a == h, o_full[h * CHUNK:(h + 1) * CHUNK], o)
            o_scr[pl.ds(r0, CHUNK), cols] = o.astype(_BF16)

        qb = qb_ref[0, pl.ds(r0, CHUNK), :]
        for kv in range(N_KV_B):
            lanes_kv = slice(kv * LANES, (kv + 1) * LANES)
            k_band = kb_all[pl.ds(r0, BAND_B), lanes_kv]
            v_band = vb_all[pl.ds(r0, BAND_B), lanes_kv]
            rows = []
            for j in (2 * kv, 2 * kv + 1):
                qc = qb[:, j * LANES:(j + 1) * LANES]
                rows.append(jnp.where(low_half, qc, jnp.zeros((), _BF16)))
                rows.append(jnp.where(low_half, jnp.zeros((), _BF16), qc))
            lhs = jnp.concatenate(rows, axis=0)
            s = lax.dot_general(lhs, k_band, _NT, preferred_element_type=_F32)
            s = jnp.where(col_b < inv_b, NEG_INF, s)
            sink = sink_cols[kv]
            m = jnp.maximum(jnp.max(s, axis=-1, keepdims=True), sink)
            e = jnp.exp(s - m)
            denom = jnp.sum(e, axis=-1, keepdims=True) + jnp.exp(sink - m)
            o_full = jnp.dot(e.astype(_BF16), v_band, preferred_element_type=_F32) / denom
            for jj in range(2):
                o = jnp.where(low_half, o_full[(2 * jj) * CHUNK:(2 * jj + 1) * CHUNK],
                              o_full[(2 * jj + 1) * CHUNK:(2 * jj + 2) * CHUNK])
                j = 2 * kv + jj
                o_scr[pl.ds(r0, CHUNK), W_A + j * LANES:W_A + (j + 1) * LANES] = o.astype(_BF16)
        return carry

    lax.fori_loop(0, n_tok // CHUNK, chunk_body, 0)

    mixed = jnp.dot(o_scr[...], wout_ref[...], preferred_element_type=_F32)
    h2 = h1_ref[0] + _rms_norm(mixed, postm_ref[...])
    f = _swiglu(_rms_norm(h2, pre2_ref[...]).astype(_BF16), wg_ref, wu_ref, wd_ref)
    h3 = h2 + 0.5 * _rms_norm(f, post2_ref[...])
    out_ref[0] = _rms_norm(h3, final_ref[...])


def _resident(shape):
    return pl.BlockSpec(shape, lambda b, s: (0,) * len(shape), pipeline_mode=pl.Buffered(1))


def _ffn_proj_call(x, pos, freq, pre1, wg, wu, wd, post1, prem, win):
    batch, seq, d = x.shape
    t = TOKENS_PER_STEP
    tok = lambda w: pl.BlockSpec((1, t, w), lambda b, s: (b, s, 0))
    out_shapes = [jax.ShapeDtypeStruct((batch, seq, d), _F32)] + [
        jax.ShapeDtypeStruct((batch, seq, w), _BF16) for w in (W_A, W_A, W_A, W_QB, W_KVB_DUP, W_KVB_DUP)]
    return pl.pallas_call(
        _ffn_proj_kernel,
        grid=(batch, seq // t),
        in_specs=[tok(d), tok(1), _resident((1, LANES)), _resident((1, d)),
                  _resident((d, D_FF)), _resident((d, D_FF)), _resident((D_FF, d)),
                  _resident((1, d)), _resident((1, d)), _resident((d, D_IN_EXT))],
        out_specs=[tok(d), tok(W_A), tok(W_A), tok(W_A), tok(W_QB), tok(W_KVB_DUP), tok(W_KVB_DUP)],
        out_shape=out_shapes,
        compiler_params=pltpu.CompilerParams(dimension_semantics=("arbitrary", "arbitrary"),
                                             vmem_limit_bytes=VMEM_LIMIT_BYTES),
        name="ffn1_proj",
    )(x, pos, freq, pre1, wg, wu, wd, post1, prem, win)


def _mixer_ffn_call(sinks, h1, qa, ka, va, qb, kb, vb, rel, wout, postm, pre2, wg, wu, wd, post2, final):
    batch, seq, d = h1.shape
    t = TOKENS_PER_STEP
    tok = lambda w: pl.BlockSpec((1, t, w), lambda b, s: (b, s, 0))
    prev_a = pl.BlockSpec((1, PAD_A, W_A), lambda b, s: (b, jnp.maximum(s * (t // PAD_A) - 1, 0), 0))
    prev_b = pl.BlockSpec((1, PAD_B, W_KVB_DUP), lambda b, s: (b, jnp.maximum(s * (t // PAD_B) - 1, 0), 0))
    return pl.pallas_call(
        _mixer_ffn_kernel,
        grid=(batch, seq // t),
        in_specs=[pl.BlockSpec(memory_space=pltpu.SMEM),
                  tok(d), tok(W_A), tok(W_QB), tok(W_A), tok(W_A), tok(W_KVB_DUP), tok(W_KVB_DUP),
                  prev_a, prev_a, prev_b, prev_b, _resident((N_HEADS_A, REL_PAD)),
                  _resident((d, d)), _resident((1, d)), _resident((1, d)),
                  _resident((d, D_FF)), _resident((d, D_FF)), _resident((D_FF, d)),
                  _resident((1, d)), _resident((1, d))],
        out_specs=tok(d),
        out_shape=jax.ShapeDtypeStruct((batch, seq, d), _F32),
        scratch_shapes=[pltpu.VMEM((PAD_A + t, W_A), _BF16), pltpu.VMEM((PAD_A + t, W_A), _BF16),
                        pltpu.VMEM((PAD_B + t, W_KVB_DUP), _BF16), pltpu.VMEM((PAD_B + t, W_KVB_DUP), _BF16),
                        pltpu.VMEM((t, 2 * W_A), _BF16),
                        pltpu.VMEM((2, 4 * CHUNK, BAND_A), _F32)],
        compiler_params=pltpu.CompilerParams(dimension_semantics=("arbitrary", "arbitrary"),
                                             vmem_limit_bytes=VMEM_LIMIT_BYTES),
        name="mixer_ffn2",
    )(sinks, h1, qa, qb, ka, va, kb, vb, ka, va, kb, vb, rel, wout, postm, pre2, wg, wu, wd, post2, final)


def kernel(x, positions, ffn1_pre_g, ffn1_w_gate, ffn1_w_up, ffn1_w_down, ffn1_post_g, mix_pre_g, w_in, rel_bias_a, sinks_b, w_out, mix_post_g, ffn2_pre_g, ffn2_w_gate, ffn2_w_up, ffn2_w_down, ffn2_post_g, final_g):
    assert x.shape[1] == SEQ and x.shape[2] == D_MODEL and ffn1_pre_g.shape[0] == 1
    row = lambda g: g[0].reshape(1, D_MODEL).astype(_F32)
    bf = lambda w: w[0].astype(_BF16)

    w = w_in[0]
    kb0 = 3 * W_A + W_QB
    vb0 = kb0 + W_KVB
    dup = lambda c0: [w[:, c0 + h * HEAD_DIM:c0 + (h + 1) * HEAD_DIM] for h in (0, 0, 1, 1)]
    win_ext = jnp.concatenate([w[:, :kb0]] + dup(kb0) + dup(vb0), axis=1).astype(_BF16)

    half = ROT_DIM // 2
    inv_freq = jnp.power(jnp.float32(ROPE_THETA), -jnp.arange(half, dtype=_F32) * (2.0 / ROT_DIM))
    lane = jnp.arange(LANES) % HEAD_DIM
    freq = jnp.where(lane < ROT_DIM, inv_freq[lane % half], 0.0).reshape(1, LANES).astype(_F32)

    rel_pad = jnp.pad(rel_bias_a[0].astype(_F32), ((0, 0), (0, REL_PAD - (2 * MAX_REL + 1))))

    h1, qa, ka, va, qb, kb, vb = _ffn_proj_call(
        x, positions[..., None], freq, row(ffn1_pre_g), bf(ffn1_w_gate), bf(ffn1_w_up), bf(ffn1_w_down),
        row(ffn1_post_g), row(mix_pre_g), win_ext)
    return _mixer_ffn_call(
        sinks_b[0].astype(_F32), h1, qa, ka, va, qb, kb, vb, rel_pad, bf(w_out),
        row(mix_post_g), row(ffn2_pre_g), bf(ffn2_w_gate), bf(ffn2_w_up), bf(ffn2_w_down),
        row(ffn2_post_g), row(final_g))
```

```python
import jax
import jax.numpy as jnp
from jax import lax
from jax.experimental import pallas as pl
from jax.experimental.pallas import tpu as pltpu

D_MODEL = 1024
SEQ = 2048
CHUNK = 64
HEAD_DIM = 64
N_HEADS_A = 8
LEFT_CHUNKS_A = 8
MAX_REL = 128
N_HEADS_B = 8
N_KV_B = 2
LEFT_CHUNKS_B = 2
ROT_DIM = HEAD_DIM // 4
ROPE_THETA = 500000.0
D_FF = 2816
EPS = 1e-6
NEG_INF = -1e30

W_A = N_HEADS_A * HEAD_DIM
W_QB = N_HEADS_B * HEAD_DIM
W_KVB = N_KV_B * HEAD_DIM
W_KVB_DUP = 2 * W_KVB
D_IN_EXT = 3 * W_A + W_QB + 2 * W_KVB_DUP
BAND_A = (LEFT_CHUNKS_A + 1) * CHUNK
BAND_B = (LEFT_CHUNKS_B + 1) * CHUNK
PAD_A = LEFT_CHUNKS_A * CHUNK
PAD_B = LEFT_CHUNKS_B * CHUNK
SCALE = HEAD_DIM ** -0.5

LANES = 128
TOEPLITZ_W = 768
REL_PAD = 384

TOKENS_PER_STEP = 512
FF_CHUNKS = tuple((lo, min(512, D_FF - lo)) for lo in range(0, D_FF, 512))
VMEM_LIMIT_BYTES = 63 * 1024 * 1024

_F32 = jnp.float32
_BF16 = jnp.bfloat16
_NT = (((1,), (1,)), ((), ()))


def _rms_norm(x, g):
    return x * lax.rsqrt(jnp.mean(x * x, axis=-1, keepdims=True) + EPS) * g


def _swiglu(xn, wg_ref, wu_ref, wd_ref):
    acc = None
    for lo, size in FF_CHUNKS:
        g = jnp.dot(xn, wg_ref[:, lo:lo + size], preferred_element_type=_F32)
        u = jnp.dot(xn, wu_ref[:, lo:lo + size], preferred_element_type=_F32)
        a = (g * (1.0 / (1.0 + jnp.exp(-g))) * u).astype(_BF16)
        part = jnp.dot(a, wd_ref[lo:lo + size, :], preferred_element_type=_F32)
        acc = part if acc is None else acc + part
    return acc


def _rope_cols(x, cos, sin_lo, sin_hi):
    return x * cos + pltpu.roll(x, ROT_DIM // 2, 1) * sin_hi + pltpu.roll(x, LANES - ROT_DIM // 2, 1) * sin_lo


def _ffn_proj_kernel(x_ref, pos_ref, freq_ref, pre1_ref, wg_ref, wu_ref, wd_ref, post1_ref, prem_ref, win_ref,
                     h1_ref, qa_ref, ka_ref, va_ref, qb_ref, kb_ref, vb_ref):
    x = x_ref[0]
    f = _swiglu(_rms_norm(x, pre1_ref[...]).astype(_BF16), wg_ref, wu_ref, wd_ref)
    h1 = x + 0.5 * _rms_norm(f, post1_ref[...])
    h1_ref[0] = h1

    u = _rms_norm(h1, prem_ref[...]).astype(_BF16)
    proj = jnp.dot(u, win_ref[...], preferred_element_type=_F32)
    qa_ref[0] = (proj[:, 0:W_A] * SCALE).astype(_BF16)
    ka_ref[0] = proj[:, W_A:2 * W_A].astype(_BF16)
    va_ref[0] = proj[:, 2 * W_A:3 * W_A].astype(_BF16)

    ang = pos_ref[0].astype(_F32) * freq_ref[...]
    cos = jnp.cos(ang)
    sin = jnp.sin(ang)
    lane = lax.broadcasted_iota(jnp.int32, ang.shape, 1) % HEAD_DIM
    sin_lo = jnp.where(lane < ROT_DIM // 2, -sin, 0.0)
    sin_hi = jnp.where(lane >= ROT_DIM // 2, sin, 0.0)
    qb0 = 3 * W_A
    for j in range(W_QB // LANES):
        blk = proj[:, qb0 + j * LANES:qb0 + (j + 1) * LANES]
        qb_ref[0, :, j * LANES:(j + 1) * LANES] = (_rope_cols(blk, cos, sin_lo, sin_hi) * SCALE).astype(_BF16)
    kb0 = qb0 + W_QB
    for j in range(W_KVB_DUP // LANES):
        blk = proj[:, kb0 + j * LANES:kb0 + (j + 1) * LANES]
        kb_ref[0, :, j * LANES:(j + 1) * LANES] = _rope_cols(blk, cos, sin_lo, sin_hi).astype(_BF16)
    vb_ref[0] = proj[:, kb0 + W_KVB_DUP:kb0 + 2 * W_KVB_DUP].astype(_BF16)


def _build_bias(rel_ref, bias_ref):
    m = lax.broadcasted_iota(jnp.int32, (REL_PAD, TOEPLITZ_W), 1)
    r = lax.broadcasted_iota(jnp.int32, (REL_PAD, TOEPLITZ_W), 0)
    sel = (r == jnp.clip(PAD_A + CHUNK - 1 - m, -MAX_REL, MAX_REL) + MAX_REL).astype(_F32)
    g = jnp.dot(rel_ref[...], sel, preferred_element_type=_F32, precision=lax.Precision.HIGHEST)
    for h in range(N_HEADS_A):
        rows = jnp.broadcast_to(g[h:h + 1, :], (CHUNK, TOEPLITZ_W))
        rolled = pltpu.roll(rows, TOEPLITZ_W - (CHUNK - 1), 1, stride=1, stride_axis=0)
        bias_ref[h // 4, (h % 4) * CHUNK:(h % 4 + 1) * CHUNK, :] = rolled[:, :BAND_A]


def _mixer_ffn_kernel(sinks_ref, h1_ref, qa_ref, qb_ref, ka_ref, va_ref, kb_ref, vb_ref,
                      rel_ref,
                      wout_ref, postm_ref, pre2_ref, wg_ref, wu_ref, wd_ref, post2_ref, final_ref,
                      out_ref, ka_all, va_all, kb_all, vb_all, o_cur, o_prev, bias_scr):
    g = pl.program_id(0)
    n_tok = TOKENS_PER_STEP
    steps_per_seq = SEQ // n_tok
    seq_step = jnp.minimum(g, pl.num_programs(0) - 2) % steps_per_seq

    @pl.when(g == 0)
    def _():
        _build_bias(rel_ref, bias_scr)
        for ref in (o_cur, ka_all, va_all, kb_all, vb_all):
            ref[...] = jnp.zeros(ref.shape, ref.dtype)

    o_prev[...] = o_cur[...]

    for hist, new, pad in ((ka_all, ka_ref, PAD_A), (va_all, va_ref, PAD_A),
                           (kb_all, kb_ref, PAD_B), (vb_all, vb_ref, PAD_B)):
        hist[0:pad] = hist[n_tok:n_tok + pad]
        hist[pad:pad + n_tok] = new[0]

    row_head = lax.broadcasted_iota(jnp.int32, (4 * CHUNK, 4 * HEAD_DIM), 0) // CHUNK
    lane_head = lax.broadcasted_iota(jnp.int32, (4 * CHUNK, 4 * HEAD_DIM), 1) // HEAD_DIM
    block_diag = row_head == lane_head
    lane_head_a = lax.broadcasted_iota(jnp.int32, (CHUNK, 4 * HEAD_DIM), 1) // HEAD_DIM
    col_a = lax.broadcasted_iota(jnp.int32, (4 * CHUNK, BAND_A), 1)
    col_b = lax.broadcasted_iota(jnp.int32, (4 * CHUNK, BAND_B), 1)
    low_half = lax.broadcasted_iota(jnp.int32, (CHUNK, LANES), 1) < HEAD_DIM
    row_blk = lax.broadcasted_iota(jnp.int32, (4 * CHUNK, 1), 0) // CHUNK
    sink_cols = []
    for kv in range(N_KV_B):
        col = jnp.zeros((4 * CHUNK, 1), _F32)
        for gq in range(4):
            col = jnp.where(row_blk == gq, sinks_ref[4 * kv + gq], col)
        sink_cols.append(col)

    for c in range(n_tok // CHUNK):
        r0 = c * CHUNK
        inv_a = jnp.where(seq_step == 0, PAD_A - r0, 0)
        inv_b = jnp.where(seq_step == 0, PAD_B - r0, 0)

        qa = qa_ref[0, r0:r0 + CHUNK, :]
        for grp in range(2):
            cols = slice(grp * 4 * HEAD_DIM, (grp + 1) * 4 * HEAD_DIM)
            q4 = qa[:, cols]
            q_bd = jnp.where(block_diag, jnp.concatenate([q4] * 4, axis=0), jnp.zeros((), _BF16))
            k_band = ka_all[r0:r0 + BAND_A, cols]
            v_band = va_all[r0:r0 + BAND_A, cols]
            s = lax.dot_general(q_bd, k_band, _NT, preferred_element_type=_F32) + bias_scr[grp]
            s = jnp.where(col_a < inv_a, NEG_INF, s)
            p = jnp.exp(s - jnp.max(s, axis=-1, keepdims=True))
            denom = jnp.sum(p, axis=-1, keepdims=True)
            o_full = jnp.dot(p.astype(_BF16), v_band, preferred_element_type=_F32) / denom
            o = jnp.zeros((CHUNK, 4 * HEAD_DIM), _F32)
            for h in range(4):
                o = jnp.where(lane_head_a == h, o_full[h * CHUNK:(h + 1) * CHUNK], o)
            o_cur[r0:r0 + CHUNK, cols] = o.astype(_BF16)

        qb = qb_ref[0, r0:r0 + CHUNK, :]
        for kv in range(N_KV_B):
            lanes_kv = slice(kv * LANES, (kv + 1) * LANES)
            k_band = kb_all[r0:r0 + BAND_B, lanes_kv]
            v_band = vb_all[r0:r0 + BAND_B, lanes_kv]
            rows = []
            for j in (2 * kv, 2 * kv + 1):
                qc = qb[:, j * LANES:(j + 1) * LANES]
                rows.append(jnp.where(low_half, qc, jnp.zeros((), _BF16)))
                rows.append(jnp.where(low_half, jnp.zeros((), _BF16), qc))
            lhs = jnp.concatenate(rows, axis=0)
            s = lax.dot_general(lhs, k_band, _NT, preferred_element_type=_F32)
            s = jnp.where(col_b < inv_b, NEG_INF, s)
            sink = sink_cols[kv]
            m = jnp.maximum(jnp.max(s, axis=-1, keepdims=True), sink)
            e = jnp.exp(s - m)
            denom = jnp.sum(e, axis=-1, keepdims=True) + jnp.exp(sink - m)
            o_full = jnp.dot(e.astype(_BF16), v_band, preferred_element_type=_F32) / denom
            for jj in range(2):
                o = jnp.where(low_half, o_full[(2 * jj) * CHUNK:(2 * jj + 1) * CHUNK],
                              o_full[(2 * jj + 1) * CHUNK:(2 * jj + 2) * CHUNK])
                j = 2 * kv + jj
                o_cur[r0:r0 + CHUNK, W_A + j * LANES:W_A + (j + 1) * LANES] = o.astype(_BF16)

    mixed = jnp.dot(o_prev[...], wout_ref[...], preferred_element_type=_F32)
    h2 = h1_ref[0] + _rms_norm(mixed, postm_ref[...])
    f = _swiglu(_rms_norm(h2, pre2_ref[...]).astype(_BF16), wg_ref, wu_ref, wd_ref)
    h3 = h2 + 0.5 * _rms_norm(f, post2_ref[...])
    out_ref[0] = _rms_norm(h3, final_ref[...])


def _resident(shape):
    return pl.BlockSpec(shape, lambda *_: (0,) * len(shape), pipeline_mode=pl.Buffered(1))


def _ffn_proj_call(x, pos, freq, pre1, wg, wu, wd, post1, prem, win):
    batch, seq, d = x.shape
    t = TOKENS_PER_STEP
    tok = lambda w: pl.BlockSpec((1, t, w), lambda b, s: (b, s, 0))
    out_shapes = [jax.ShapeDtypeStruct((batch, seq, d), _F32)] + [
        jax.ShapeDtypeStruct((batch, seq, w), _BF16) for w in (W_A, W_A, W_A, W_QB, W_KVB_DUP, W_KVB_DUP)]
    return pl.pallas_call(
        _ffn_proj_kernel,
        grid=(batch, seq // t),
        in_specs=[tok(d), tok(1), _resident((1, LANES)), _resident((1, d)),
                  _resident((d, D_FF)), _resident((d, D_FF)), _resident((D_FF, d)),
                  _resident((1, d)), _resident((1, d)), _resident((d, D_IN_EXT))],
        out_specs=[tok(d), tok(W_A), tok(W_A), tok(W_A), tok(W_QB), tok(W_KVB_DUP), tok(W_KVB_DUP)],
        out_shape=out_shapes,
        compiler_params=pltpu.CompilerParams(dimension_semantics=("arbitrary", "arbitrary"),
                                             vmem_limit_bytes=VMEM_LIMIT_BYTES),
        name="ffn1_proj",
    )(x, pos, freq, pre1, wg, wu, wd, post1, prem, win)


def _mixer_ffn_call(sinks, h1, qa, ka, va, qb, kb, vb, rel, wout, postm, pre2, wg, wu, wd, post2, final):
    batch, seq, d = h1.shape
    t = TOKENS_PER_STEP
    per_seq = seq // t
    n_tiles = batch * per_seq

    def att(w):
        def index(g):
            tile = jnp.minimum(g, n_tiles - 1)
            return (tile // per_seq, tile % per_seq, 0)
        return pl.BlockSpec((1, t, w), index)

    def tail(w):
        def index(g):
            tile = jnp.maximum(g - 1, 0)
            return (tile // per_seq, tile % per_seq, 0)
        return pl.BlockSpec((1, t, w), index)

    return pl.pallas_call(
        _mixer_ffn_kernel,
        grid=(n_tiles + 1,),
        in_specs=[pl.BlockSpec(memory_space=pltpu.SMEM),
                  tail(d), att(W_A), att(W_QB), att(W_A), att(W_A), att(W_KVB_DUP), att(W_KVB_DUP),
                  _resident((N_HEADS_A, REL_PAD)),
                  _resident((d, d)), _resident((1, d)), _resident((1, d)),
                  _resident((d, D_FF)), _resident((d, D_FF)), _resident((D_FF, d)),
                  _resident((1, d)), _resident((1, d))],
        out_specs=tail(d),
        out_shape=jax.ShapeDtypeStruct((batch, seq, d), _F32),
        scratch_shapes=[pltpu.VMEM((PAD_A + t, W_A), _BF16), pltpu.VMEM((PAD_A + t, W_A), _BF16),
                        pltpu.VMEM((PAD_B + t, W_KVB_DUP), _BF16), pltpu.VMEM((PAD_B + t, W_KVB_DUP), _BF16),
                        pltpu.VMEM((t, 2 * W_A), _BF16), pltpu.VMEM((t, 2 * W_A), _BF16),
                        pltpu.VMEM((2, 4 * CHUNK, BAND_A), _F32)],
        compiler_params=pltpu.CompilerParams(dimension_semantics=("arbitrary",),
                                             vmem_limit_bytes=VMEM_LIMIT_BYTES),
        name="mixer_ffn2",
    )(sinks, h1, qa, qb, ka, va, kb, vb, rel, wout, postm, pre2, wg, wu, wd, post2, final)


def kernel(x, positions, ffn1_pre_g, ffn1_w_gate, ffn1_w_up, ffn1_w_down, ffn1_post_g, mix_pre_g, w_in, rel_bias_a, sinks_b, w_out, mix_post_g, ffn2_pre_g, ffn2_w_gate, ffn2_w_up, ffn2_w_down, ffn2_post_g, final_g):
    assert x.shape[1] == SEQ and x.shape[2] == D_MODEL and ffn1_pre_g.shape[0] == 1
    row = lambda g: g[0].reshape(1, D_MODEL).astype(_F32)
    bf = lambda w: w[0].astype(_BF16)

    w = w_in[0]
    kb0 = 3 * W_A + W_QB
    vb0 = kb0 + W_KVB
    dup = lambda c0: [w[:, c0 + h * HEAD_DIM:c0 + (h + 1) * HEAD_DIM] for h in (0, 0, 1, 1)]
    win_ext = jnp.concatenate([w[:, :kb0]] + dup(kb0) + dup(vb0), axis=1).astype(_BF16)

    half = ROT_DIM // 2
    inv_freq = jnp.power(jnp.float32(ROPE_THETA), -jnp.arange(half, dtype=_F32) * (2.0 / ROT_DIM))
    lane = jnp.arange(LANES) % HEAD_DIM
    freq = jnp.where(lane < ROT_DIM, inv_freq[lane % half], 0.0).reshape(1, LANES).astype(_F32)

    rel_pad = jnp.pad(rel_bias_a[0].astype(_F32), ((0, 0), (0, REL_PAD - (2 * MAX_REL + 1))))

    h1, qa, ka, va, qb, kb, vb = _ffn_proj_call(
        x, positions[..., None], freq, row(ffn1_pre_g), bf(ffn1_w_gate), bf(ffn1_w_up), bf(ffn1_w_down),
        row(ffn1_post_g), row(mix_pre_g), win_ext)
    return _mixer_ffn_call(
        sinks_b[0].astype(_F32), h1, qa, ka, va, qb, kb, vb, rel_pad, bf(w_out),
        row(mix_post_g), row(ffn2_pre_g), bf(ffn2_w_gate), bf(ffn2_w_up), bf(ffn2_w_down),
        row(ffn2_post_g), row(final_g))
```

```python
import jax
import jax.numpy as jnp
from jax import lax
from jax.experimental import pallas as pl
from jax.experimental.pallas import tpu as pltpu

D_MODEL = 1024
SEQ = 2048
CHUNK = 64
HEAD_DIM = 64
N_HEADS_A = 8
LEFT_CHUNKS_A = 8
MAX_REL = 128
N_HEADS_B = 8
N_KV_B = 2
LEFT_CHUNKS_B = 2
ROT_DIM = HEAD_DIM // 4
ROPE_THETA = 500000.0
D_FF = 2816
EPS = 1e-6
NEG_INF = -1e30

W_A = N_HEADS_A * HEAD_DIM
W_QB = N_HEADS_B * HEAD_DIM
W_KVB = N_KV_B * HEAD_DIM
W_KVB_DUP = 2 * W_KVB
D_IN_EXT = 3 * W_A + W_QB + 2 * W_KVB_DUP
BAND_A = (LEFT_CHUNKS_A + 1) * CHUNK
BAND_B = (LEFT_CHUNKS_B + 1) * CHUNK
PAD_A = LEFT_CHUNKS_A * CHUNK
PAD_B = LEFT_CHUNKS_B * CHUNK
SCALE = HEAD_DIM ** -0.5

LANES = 128
TOEPLITZ_W = 768
REL_PAD = 384

TOKENS_PER_STEP = 512
FF_CHUNKS = tuple((lo, min(512, D_FF - lo)) for lo in range(0, D_FF, 512))
VMEM_LIMIT_BYTES = 63 * 1024 * 1024

_F32 = jnp.float32
_BF16 = jnp.bfloat16
_NT = (((1,), (1,)), ((), ()))


def _rms_norm(x, g):
    return x * lax.rsqrt(jnp.mean(x * x, axis=-1, keepdims=True) + EPS) * g


def _swiglu(xn, wg_ref, wu_ref, wd_ref):
    acc = None
    for lo, size in FF_CHUNKS:
        g = jnp.dot(xn, wg_ref[:, lo:lo + size], preferred_element_type=_F32)
        u = jnp.dot(xn, wu_ref[:, lo:lo + size], preferred_element_type=_F32)
        a = (g * (1.0 / (1.0 + jnp.exp(-g))) * u).astype(_BF16)
        part = jnp.dot(a, wd_ref[lo:lo + size, :], preferred_element_type=_F32)
        acc = part if acc is None else acc + part
    return acc


def _rope_cols(x, cos, sin_lo, sin_hi):
    return x * cos + pltpu.roll(x, ROT_DIM // 2, 1) * sin_hi + pltpu.roll(x, LANES - ROT_DIM // 2, 1) * sin_lo


def _ffn_proj_kernel(x_ref, pos_ref, freq_ref, pre1_ref, wg_ref, wu_ref, wd_ref, post1_ref, prem_ref, win_ref,
                     h1_ref, qa_ref, ka_ref, va_ref, qb_ref, kb_ref, vb_ref):
    x = x_ref[0]
    f = _swiglu(_rms_norm(x, pre1_ref[...]).astype(_BF16), wg_ref, wu_ref, wd_ref)
    h1 = x + 0.5 * _rms_norm(f, post1_ref[...])
    h1_ref[0] = h1

    u = _rms_norm(h1, prem_ref[...]).astype(_BF16)
    proj = jnp.dot(u, win_ref[...], preferred_element_type=_F32)
    qa_ref[0] = (proj[:, 0:W_A] * SCALE).astype(_BF16)
    ka_ref[0] = proj[:, W_A:2 * W_A].astype(_BF16)
    va_ref[0] = proj[:, 2 * W_A:3 * W_A].astype(_BF16)

    ang = pos_ref[0].astype(_F32) * freq_ref[...]
    cos = jnp.cos(ang)
    sin = jnp.sin(ang)
    lane = lax.broadcasted_iota(jnp.int32, ang.shape, 1) % HEAD_DIM
    sin_lo = jnp.where(lane < ROT_DIM // 2, -sin, 0.0)
    sin_hi = jnp.where(lane >= ROT_DIM // 2, sin, 0.0)
    qb0 = 3 * W_A
    for j in range(W_QB // LANES):
        blk = proj[:, qb0 + j * LANES:qb0 + (j + 1) * LANES]
        qb_ref[0, :, j * LANES:(j + 1) * LANES] = (_rope_cols(blk, cos, sin_lo, sin_hi) * SCALE).astype(_BF16)
    kb0 = qb0 + W_QB
    for j in range(W_KVB_DUP // LANES):
        blk = proj[:, kb0 + j * LANES:kb0 + (j + 1) * LANES]
        kb_ref[0, :, j * LANES:(j + 1) * LANES] = _rope_cols(blk, cos, sin_lo, sin_hi).astype(_BF16)
    vb_ref[0] = proj[:, kb0 + W_KVB_DUP:kb0 + 2 * W_KVB_DUP].astype(_BF16)


def _bias_kernel(rel_ref, bias_ref):
    m = lax.broadcasted_iota(jnp.int32, (REL_PAD, TOEPLITZ_W), 1)
    r = lax.broadcasted_iota(jnp.int32, (REL_PAD, TOEPLITZ_W), 0)
    sel = (r == jnp.clip(PAD_A + CHUNK - 1 - m, -MAX_REL, MAX_REL) + MAX_REL).astype(_F32)
    g = jnp.dot(rel_ref[...], sel, preferred_element_type=_F32, precision=lax.Precision.HIGHEST)
    for h in range(N_HEADS_A):
        rows = jnp.broadcast_to(g[h:h + 1, :], (CHUNK, TOEPLITZ_W))
        rolled = pltpu.roll(rows, TOEPLITZ_W - (CHUNK - 1), 1, stride=1, stride_axis=0)
        bias_ref[h // 4, (h % 4) * CHUNK:(h % 4 + 1) * CHUNK, :] = rolled[:, :BAND_A]


def _zero_token(x, zero):
    return pltpu.bitcast(pltpu.bitcast(x[0:8, 0:LANES], jnp.int32) & zero, _F32)


def _after(x, token):
    if token is None:
        return x
    top = x[0:8, :] + jnp.tile(token, (1, x.shape[1] // LANES))
    return jnp.concatenate([top, x[8:, :]], axis=0)


def _mixer_ffn_kernel(sinks_ref, zero_ref, h1_ref, qa_ref, qb_ref, ka_ref, va_ref, kb_ref, vb_ref, bias_ref,
                      wout_ref, postm_ref, pre2_ref, wg_ref, wu_ref, wd_ref, post2_ref, final_ref,
                      out_ref, ka_all, va_all, kb_all, vb_all, o_scr):
    g = pl.program_id(0)
    n_tok = TOKENS_PER_STEP
    steps_per_seq = SEQ // n_tok
    seq_step = jnp.minimum(g, pl.num_programs(0) - 2) % steps_per_seq

    @pl.when(g == 0)
    def _():
        for ref in (o_scr, ka_all, va_all, kb_all, vb_all):
            ref[...] = jnp.zeros(ref.shape, ref.dtype)

    def dense_half(rows):
        st = {}

        def wout():
            st["mixed"] = jnp.dot(o_scr[rows, :], wout_ref[...], preferred_element_type=_F32)

        def norm():
            st["h2"] = h1_ref[0, rows, :] + _rms_norm(st.pop("mixed"), postm_ref[...])
            st["xn"] = _rms_norm(st["h2"], pre2_ref[...]).astype(_BF16)

        def gate(lo, size):
            st["g"] = jnp.dot(st["xn"], wg_ref[:, lo:lo + size], preferred_element_type=_F32)

        def up(lo, size):
            st["u"] = jnp.dot(st["xn"], wu_ref[:, lo:lo + size], preferred_element_type=_F32)

        def down(lo, size, token):
            gt = _after(st.pop("g"), token)
            a = (gt * (1.0 / (1.0 + jnp.exp(-gt))) * st.pop("u")).astype(_BF16)
            part = jnp.dot(a, wd_ref[lo:lo + size, :], preferred_element_type=_F32)
            st["acc"] = part if "acc" not in st else st["acc"] + part

        def final():
            h3 = st["h2"] + 0.5 * _rms_norm(st["acc"], post2_ref[...])
            out_ref[0, rows, :] = _rms_norm(h3, final_ref[...])

        return wout, norm, gate, up, down, final

    half_rows = n_tok // 2
    dense = [dense_half(slice(0, half_rows)), dense_half(slice(half_rows, n_tok))]

    for hist, new, pad in ((ka_all, ka_ref, PAD_A), (va_all, va_ref, PAD_A),
                           (kb_all, kb_ref, PAD_B), (vb_all, vb_ref, PAD_B)):
        hist[0:pad] = hist[n_tok:n_tok + pad]
        hist[pad:pad + n_tok] = new[0]

    row_head = lax.broadcasted_iota(jnp.int32, (4 * CHUNK, 4 * HEAD_DIM), 0) // CHUNK
    lane_head = lax.broadcasted_iota(jnp.int32, (4 * CHUNK, 4 * HEAD_DIM), 1) // HEAD_DIM
    block_diag = row_head == lane_head
    lane_head_a = lax.broadcasted_iota(jnp.int32, (CHUNK, 4 * HEAD_DIM), 1) // HEAD_DIM
    col_a = lax.broadcasted_iota(jnp.int32, (4 * CHUNK, BAND_A), 1)
    col_b = lax.broadcasted_iota(jnp.int32, (4 * CHUNK, BAND_B), 1)
    low_half = lax.broadcasted_iota(jnp.int32, (CHUNK, LANES), 1) < HEAD_DIM
    row_blk = lax.broadcasted_iota(jnp.int32, (4 * CHUNK, 1), 0) // CHUNK
    sink_cols = []
    for kv in range(N_KV_B):
        col = jnp.zeros((4 * CHUNK, 1), _F32)
        for gq in range(4):
            col = jnp.where(row_blk == gq, sinks_ref[4 * kv + gq], col)
        sink_cols.append(col)

    def attn_a(c, grp):
        r0 = c * CHUNK
        cols = slice(grp * 4 * HEAD_DIM, (grp + 1) * 4 * HEAD_DIM)
        st = {}

        def scores():
            inv = jnp.where(seq_step == 0, PAD_A - r0, 0)
            q4 = qa_ref[0, r0:r0 + CHUNK, cols]
            q_bd = jnp.where(block_diag, jnp.concatenate([q4] * 4, axis=0), jnp.zeros((), _BF16))
            k_band = ka_all[r0:r0 + BAND_A, cols]
            s = lax.dot_general(q_bd, k_band, _NT, preferred_element_type=_F32) + bias_ref[grp]
            s = jnp.where(col_a < inv, NEG_INF, s)
            p = jnp.exp(s - jnp.max(s, axis=-1, keepdims=True))
            st["p"] = (p / jnp.sum(p, axis=-1, keepdims=True)).astype(_BF16)

        def values():
            v_band = va_all[r0:r0 + BAND_A, cols]
            o_full = jnp.dot(st.pop("p"), v_band, preferred_element_type=_F32)
            o = jnp.zeros((CHUNK, 4 * HEAD_DIM), _F32)
            for h in range(4):
                o = jnp.where(lane_head_a == h, o_full[h * CHUNK:(h + 1) * CHUNK], o)
            o_scr[r0:r0 + CHUNK, cols] = o.astype(_BF16)
            return _zero_token(o, zero_ref[0])

        return scores, values

    def attn_b(c, kv):
        r0 = c * CHUNK
        lanes_kv = slice(kv * LANES, (kv + 1) * LANES)
        st = {}

        def scores():
            inv = jnp.where(seq_step == 0, PAD_B - r0, 0)
            k_band = kb_all[r0:r0 + BAND_B, lanes_kv]
            rows = []
            for j in (2 * kv, 2 * kv + 1):
                qc = qb_ref[0, r0:r0 + CHUNK, j * LANES:(j + 1) * LANES]
                rows.append(jnp.where(low_half, qc, jnp.zeros((), _BF16)))
                rows.append(jnp.where(low_half, jnp.zeros((), _BF16), qc))
            lhs = jnp.concatenate(rows, axis=0)
            s = lax.dot_general(lhs, k_band, _NT, preferred_element_type=_F32)
            s = jnp.where(col_b < inv, NEG_INF, s)
            sink = sink_cols[kv]
            m = jnp.maximum(jnp.max(s, axis=-1, keepdims=True), sink)
            e = jnp.exp(s - m)
            denom = jnp.sum(e, axis=-1, keepdims=True) + jnp.exp(sink - m)
            st["e"] = (e / denom).astype(_BF16)

        def values():
            v_band = vb_all[r0:r0 + BAND_B, lanes_kv]
            o_full = jnp.dot(st.pop("e"), v_band, preferred_element_type=_F32)
            for jj in range(2):
                o = jnp.where(low_half, o_full[(2 * jj) * CHUNK:(2 * jj + 1) * CHUNK],
                              o_full[(2 * jj + 1) * CHUNK:(2 * jj + 2) * CHUNK])
                j = 2 * kv + jj
                o_scr[r0:r0 + CHUNK, W_A + j * LANES:W_A + (j + 1) * LANES] = o.astype(_BF16)
            return _zero_token(o, zero_ref[0])

        return scores, values

    attn = []
    for c in range(n_tok // CHUNK):
        attn += [attn_a(c, 0), attn_a(c, 1), attn_b(c, 0), attn_b(c, 1)]

    tokens = []
    started = [0]

    def attention_slot():
        j = started[0]
        if j < len(attn):
            attn[j][0]()
        if 0 < j <= len(attn):
            tokens.append(attn[j - 1][1]())
        started[0] = j + 1

    def take_token():
        token = None
        while len(tokens) > 1:
            zero = tokens.pop(0)
            token = zero if token is None else token + zero
        return token

    dense[0][0]()
    attention_slot()
    dense[1][0]()
    attention_slot()
    dense[0][1]()
    attention_slot()
    first_lo, last_lo = FF_CHUNKS[0][0], FF_CHUNKS[-1][0]
    for lo, size in FF_CHUNKS:
        for half, (_, norm, gate, up, _, _) in enumerate(dense):
            if lo == first_lo and half == 1:
                norm()
                attention_slot()
            gate(lo, size)
            attention_slot()
            up(lo, size)
            attention_slot()
        for _, _, _, _, down, final in dense:
            down(lo, size, take_token())
            attention_slot()
            if lo == last_lo:
                final()
    assert started[0] > len(attn)


def _bias_call(rel):
    return pl.pallas_call(
        _bias_kernel,
        out_shape=jax.ShapeDtypeStruct((2, 4 * CHUNK, BAND_A), _F32),
        name="rel_bias_table",
    )(rel)


def _resident(shape):
    return pl.BlockSpec(shape, lambda *_: (0,) * len(shape), pipeline_mode=pl.Buffered(1))


def _ffn_proj_call(x, pos, freq, pre1, wg, wu, wd, post1, prem, win):
    batch, seq, d = x.shape
    t = TOKENS_PER_STEP
    tok = lambda w: pl.BlockSpec((1, t, w), lambda b, s: (b, s, 0))
    out_shapes = [jax.ShapeDtypeStruct((batch, seq, d), _F32)] + [
        jax.ShapeDtypeStruct((batch, seq, w), _BF16) for w in (W_A, W_A, W_A, W_QB, W_KVB_DUP, W_KVB_DUP)]
    return pl.pallas_call(
        _ffn_proj_kernel,
        grid=(batch, seq // t),
        in_specs=[tok(d), tok(1), _resident((1, LANES)), _resident((1, d)),
                  _resident((d, D_FF)), _resident((d, D_FF)), _resident((D_FF, d)),
                  _resident((1, d)), _resident((1, d)), _resident((d, D_IN_EXT))],
        out_specs=[tok(d), tok(W_A), tok(W_A), tok(W_A), tok(W_QB), tok(W_KVB_DUP), tok(W_KVB_DUP)],
        out_shape=out_shapes,
        compiler_params=pltpu.CompilerParams(dimension_semantics=("arbitrary", "arbitrary"),
                                             vmem_limit_bytes=VMEM_LIMIT_BYTES),
        name="ffn1_proj",
    )(x, pos, freq, pre1, wg, wu, wd, post1, prem, win)


def _mixer_ffn_call(sinks, h1, qa, ka, va, qb, kb, vb, bias, wout, postm, pre2, wg, wu, wd, post2, final):
    batch, seq, d = h1.shape
    t = TOKENS_PER_STEP
    per_seq = seq // t
    n_tiles = batch * per_seq

    def att(w):
        def index(g):
            tile = jnp.minimum(g, n_tiles - 1)
            return (tile // per_seq, tile % per_seq, 0)
        return pl.BlockSpec((1, t, w), index)

    def tail(w):
        def index(g):
            tile = jnp.maximum(g - 1, 0)
            return (tile // per_seq, tile % per_seq, 0)
        return pl.BlockSpec((1, t, w), index)

    smem = pl.BlockSpec(memory_space=pltpu.SMEM)
    run_time_zero = jnp.zeros((1,), jnp.int32)
    return pl.pallas_call(
        _mixer_ffn_kernel,
        grid=(n_tiles + 1,),
        in_specs=[smem, smem,
                  tail(d), att(W_A), att(W_QB), att(W_A), att(W_A), att(W_KVB_DUP), att(W_KVB_DUP),
                  _resident((2, 4 * CHUNK, BAND_A)),
                  _resident((d, d)), _resident((1, d)), _resident((1, d)),
                  _resident((d, D_FF)), _resident((d, D_FF)), _resident((D_FF, d)),
                  _resident((1, d)), _resident((1, d))],
        out_specs=tail(d),
        out_shape=jax.ShapeDtypeStruct((batch, seq, d), _F32),
        scratch_shapes=[pltpu.VMEM((PAD_A + t, W_A), _BF16), pltpu.VMEM((PAD_A + t, W_A), _BF16),
                        pltpu.VMEM((PAD_B + t, W_KVB_DUP), _BF16), pltpu.VMEM((PAD_B + t, W_KVB_DUP), _BF16),
                        pltpu.VMEM((t, 2 * W_A), _BF16)],
        compiler_params=pltpu.CompilerParams(dimension_semantics=("arbitrary",),
                                             vmem_limit_bytes=VMEM_LIMIT_BYTES),
        name="mixer_ffn2",
    )(sinks, run_time_zero, h1, qa, qb, ka, va, kb, vb, bias, wout, postm, pre2, wg, wu, wd, post2, final)


def kernel(x, positions, ffn1_pre_g, ffn1_w_gate, ffn1_w_up, ffn1_w_down, ffn1_post_g, mix_pre_g, w_in, rel_bias_a, sinks_b, w_out, mix_post_g, ffn2_pre_g, ffn2_w_gate, ffn2_w_up, ffn2_w_down, ffn2_post_g, final_g):
    assert x.shape[1] == SEQ and x.shape[2] == D_MODEL and ffn1_pre_g.shape[0] == 1
    row = lambda g: g[0].reshape(1, D_MODEL).astype(_F32)
    bf = lambda w: w[0].astype(_BF16)

    w = w_in[0]
    kb0 = 3 * W_A + W_QB
    vb0 = kb0 + W_KVB
    dup = lambda c0: [w[:, c0 + h * HEAD_DIM:c0 + (h + 1) * HEAD_DIM] for h in (0, 0, 1, 1)]
    win_ext = jnp.concatenate([w[:, :kb0]] + dup(kb0) + dup(vb0), axis=1).astype(_BF16)

    half = ROT_DIM // 2
    inv_freq = jnp.power(jnp.float32(ROPE_THETA), -jnp.arange(half, dtype=_F32) * (2.0 / ROT_DIM))
    lane = jnp.arange(LANES) % HEAD_DIM
    freq = jnp.where(lane < ROT_DIM, inv_freq[lane % half], 0.0).reshape(1, LANES).astype(_F32)

    rel_pad = jnp.pad(rel_bias_a[0].astype(_F32), ((0, 0), (0, REL_PAD - (2 * MAX_REL + 1))))

    h1, qa, ka, va, qb, kb, vb = _ffn_proj_call(
        x, positions[..., None], freq, row(ffn1_pre_g), bf(ffn1_w_gate), bf(ffn1_w_up), bf(ffn1_w_down),
        row(ffn1_post_g), row(mix_pre_g), win_ext)
    return _mixer_ffn_call(
        sinks_b[0].astype(_F32), h1, qa, ka, va, qb, kb, vb, _bias_call(rel_pad), bf(w_out),
        row(mix_post_g), row(ffn2_pre_g), bf(ffn2_w_gate), bf(ffn2_w_up), bf(ffn2_w_down),
        row(ffn2_post_g), row(final_g))
```

```python
import jax
import jax.numpy as jnp
from jax import lax
from jax.experimental import pallas as pl
from jax.experimental.pallas import tpu as pltpu

D_MODEL = 1024
SEQ = 2048
CHUNK = 64
HEAD_DIM = 64
N_HEADS_A = 8
LEFT_CHUNKS_A = 8
MAX_REL = 128
N_HEADS_B = 8
N_KV_B = 2
LEFT_CHUNKS_B = 2
ROT_DIM = HEAD_DIM // 4
ROPE_THETA = 500000.0
D_FF = 2816
EPS = 1e-6
NEG_INF = -1e30

W_A = N_HEADS_A * HEAD_DIM
W_QB = N_HEADS_B * HEAD_DIM
W_KVB = N_KV_B * HEAD_DIM
W_KVB_DUP = 2 * W_KVB
D_IN = 3 * W_A + W_QB + 2 * W_KVB
BAND_A = (LEFT_CHUNKS_A + 1) * CHUNK
BAND_B = (LEFT_CHUNKS_B + 1) * CHUNK
PAD_A = LEFT_CHUNKS_A * CHUNK
PAD_B = LEFT_CHUNKS_B * CHUNK
SCALE = HEAD_DIM ** -0.5

LANES = 128
TOEPLITZ_W = 768
REL_PAD = 384

TOKENS_PER_STEP = 512
FF_CHUNKS = tuple((lo, min(512, D_FF - lo)) for lo in range(0, D_FF, 512))
VMEM_LIMIT_BYTES = 63 * 1024 * 1024

_F32 = jnp.float32
_BF16 = jnp.bfloat16
_NT = (((1,), (1,)), ((), ()))


def _rms_norm(x, g):
    return x * lax.rsqrt(jnp.mean(x * x, axis=-1, keepdims=True) + EPS) * g


def _rope_cols(x, cos, sin_lo, sin_hi):
    return x * cos + pltpu.roll(x, ROT_DIM // 2, 1) * sin_hi + pltpu.roll(x, LANES - ROT_DIM // 2, 1) * sin_lo


def _zero_token(x, zero):
    return pltpu.bitcast(pltpu.bitcast(x[0:8, 0:LANES], jnp.int32) & zero, _F32)


def _after(x, token):
    if token is None:
        return x
    top = x[0:8, :] + jnp.tile(token, (1, x.shape[1] // LANES))
    return jnp.concatenate([top, x[8:, :]], axis=0)


def _ffn_proj_kernel(zero_ref, x_ref, pos_ref, freq_ref, pre1_ref, wg_ref, wu_ref, wd_ref, post1_ref, prem_ref,
                     win_ref, h1_ref, qa_ref, ka_ref, va_ref, qb_ref, kb_ref, vb_ref, u_scr):
    n_tok = TOKENS_PER_STEP
    half_rows = n_tok // 2

    @pl.when(pl.program_id(0) == 0)
    def _():
        u_scr[...] = jnp.zeros(u_scr.shape, u_scr.dtype)

    def proj_half(rows):
        st = {}

        def tables():
            n = rows.stop - rows.start
            ang = freq_ref[...] * pos_ref[0][:, rows].astype(_F32)
            packed = jnp.concatenate([jnp.cos(ang), jnp.sin(ang), jnp.zeros((LANES - ROT_DIM, n), _F32)], axis=0)
            cs = packed.T
            lane = lax.broadcasted_iota(jnp.int32, cs.shape, 1)
            in_head = lane % HEAD_DIM
            first_head = lane < HEAD_DIM
            half = ROT_DIM // 2

            def both_heads(t):
                return jnp.where(first_head, t, pltpu.roll(t, HEAD_DIM, 1))

            cos = both_heads(jnp.where(lane < half, cs, pltpu.roll(cs, half, 1)))
            sin_lo = both_heads(pltpu.roll(cs, LANES - half, 1))
            sin_hi = both_heads(cs)
            st["cos"] = jnp.where(in_head < ROT_DIM, cos, 1.0)
            st["sin_lo"] = jnp.where(in_head < half, -sin_lo, 0.0)
            st["sin_hi"] = jnp.where((in_head >= half) & (in_head < ROT_DIM), sin_hi, 0.0)
            st["low"] = first_head
            return _zero_token(st["cos"] + st["sin_hi"], zero_ref[0])

        def cols(c0, width):
            return jnp.dot(u_scr[rows, :], win_ref[:, c0:c0 + width], preferred_element_type=_F32)

        def rope(blk):
            return _rope_cols(blk, st["cos"], st["sin_lo"], st["sin_hi"])

        def twice(blk):
            swapped = pltpu.roll(blk, HEAD_DIM, 1)
            return jnp.concatenate([jnp.where(st["low"], blk, swapped), jnp.where(st["low"], swapped, blk)], axis=1)

        def qa():
            q = cols(0, W_A) * SCALE
            qa_ref[0, rows, :] = q.astype(_BF16)
            return _zero_token(q, zero_ref[0])

        def ka():
            k = cols(W_A, W_A)
            ka_ref[0, rows, :] = k.astype(_BF16)
            return _zero_token(k, zero_ref[0])

        def va():
            v = cols(2 * W_A, W_A)
            va_ref[0, rows, :] = v.astype(_BF16)
            return _zero_token(v, zero_ref[0])

        def qb():
            p = cols(3 * W_A, W_QB)
            for j in range(W_QB // LANES):
                q = rope(p[:, j * LANES:(j + 1) * LANES]) * SCALE
                qb_ref[0, rows, j * LANES:(j + 1) * LANES] = q.astype(_BF16)
            return _zero_token(q, zero_ref[0])

        def kvb():
            p = cols(3 * W_A + W_QB, 2 * W_KVB)
            k = twice(rope(p[:, 0:W_KVB]))
            kb_ref[0, rows, :] = k.astype(_BF16)
            vb_ref[0, rows, :] = twice(p[:, W_KVB:2 * W_KVB]).astype(_BF16)
            return _zero_token(k, zero_ref[0])

        return tables, qa, ka, va, qb, kvb

    def ffn_half(rows):
        st = {}

        def norm():
            st["x"] = x_ref[0, rows, :]
            st["xn"] = _rms_norm(st["x"], pre1_ref[...]).astype(_BF16)

        def gate(lo, size):
            st["g"] = jnp.dot(st["xn"], wg_ref[:, lo:lo + size], preferred_element_type=_F32)

        def up(lo, size):
            st["u"] = jnp.dot(st["xn"], wu_ref[:, lo:lo + size], preferred_element_type=_F32)

        def down(lo, size, token):
            gt = _after(st.pop("g"), token)
            a = (gt * (1.0 / (1.0 + jnp.exp(-gt))) * st.pop("u")).astype(_BF16)
            part = jnp.dot(a, wd_ref[lo:lo + size, :], preferred_element_type=_F32)
            st["acc"] = part if "acc" not in st else st["acc"] + part

        def final():
            h1 = st["x"] + 0.5 * _rms_norm(st["acc"], post1_ref[...])
            h1_ref[0, rows, :] = h1
            st["u_next"] = _rms_norm(h1, prem_ref[...]).astype(_BF16)

        def carry():
            u_scr[rows, :] = st.pop("u_next")

        return norm, gate, up, down, final, carry

    halves = (slice(0, half_rows), slice(half_rows, n_tok))
    ffn = [ffn_half(rows) for rows in halves]
    tables0, qa0, ka0, va0, qb0, kvb0 = proj_half(halves[0])
    tables1, qa1, ka1, va1, qb1, kvb1 = proj_half(halves[1])
    proj_pieces = [qa0, ka0, va0, qa1, qb0, kvb0, qb1, kvb1, ka1, va1]
    held_back = 2
    tokens = []

    def proj_slot():
        if len(proj_pieces) > held_back:
            tokens.append(proj_pieces.pop(0)())

    def take_token(lag=2):
        token = None
        while len(tokens) > lag:
            zero = tokens.pop(0)
            token = zero if token is None else token + zero
        return token

    ffn[0][0]()
    proj_slot()
    ffn[1][0]()
    proj_slot()
    tokens.append(tables0())
    proj_slot()
    tokens.append(tables1())
    last_lo = FF_CHUNKS[-1][0]
    for lo, size in FF_CHUNKS:
        for _, gate, up, _, _, _ in ffn:
            gate(lo, size)
            up(lo, size)
        for _, _, _, down, final, _ in ffn:
            down(lo, size, take_token())
            proj_slot()
            if lo == last_lo:
                final()
    while proj_pieces:
        proj_pieces.pop(0)()
    for *_, carry in ffn:
        carry()


def _bias_kernel(rel_ref, bias_ref):
    m = lax.broadcasted_iota(jnp.int32, (REL_PAD, TOEPLITZ_W), 1)
    r = lax.broadcasted_iota(jnp.int32, (REL_PAD, TOEPLITZ_W), 0)
    sel = (r == jnp.clip(PAD_A + CHUNK - 1 - m, -MAX_REL, MAX_REL) + MAX_REL).astype(_F32)
    g = jnp.dot(rel_ref[...], sel, preferred_element_type=_F32, precision=lax.Precision.HIGHEST)
    for h in range(N_HEADS_A):
        rows = jnp.broadcast_to(g[h:h + 1, :], (CHUNK, TOEPLITZ_W))
        rolled = pltpu.roll(rows, TOEPLITZ_W - (CHUNK - 1), 1, stride=1, stride_axis=0)
        bias_ref[h // 4, (h % 4) * CHUNK:(h % 4 + 1) * CHUNK, :] = rolled[:, :BAND_A]


def _mixer_ffn_kernel(sinks_ref, zero_ref, h1_ref, qa_ref, qb_ref, ka_ref, va_ref, kb_ref, vb_ref, bias_ref,
                      wout_ref, postm_ref, pre2_ref, wg_ref, wu_ref, wd_ref, post2_ref, final_ref,
                      out_ref, ka_all, va_all, kb_all, vb_all, o_scr):
    g = pl.program_id(0)
    n_tok = TOKENS_PER_STEP
    steps_per_seq = SEQ // n_tok
    seq_step = jnp.minimum(g, pl.num_programs(0) - 2) % steps_per_seq

    @pl.when(g == 0)
    def _():
        for ref in (o_scr, ka_all, va_all, kb_all, vb_all):
            ref[...] = jnp.zeros(ref.shape, ref.dtype)

    def dense_half(rows):
        st = {}

        def wout():
            st["mixed"] = jnp.dot(o_scr[rows, :], wout_ref[...], preferred_element_type=_F32)

        def norm():
            st["h2"] = h1_ref[0, rows, :] + _rms_norm(st.pop("mixed"), postm_ref[...])
            st["xn"] = _rms_norm(st["h2"], pre2_ref[...]).astype(_BF16)

        def gate(lo, size):
            st["g"] = jnp.dot(st["xn"], wg_ref[:, lo:lo + size], preferred_element_type=_F32)

        def up(lo, size):
            st["u"] = jnp.dot(st["xn"], wu_ref[:, lo:lo + size], preferred_element_type=_F32)

        def down(lo, size, token):
            gt = _after(st.pop("g"), token)
            a = (gt * (1.0 / (1.0 + jnp.exp(-gt))) * st.pop("u")).astype(_BF16)
            part = jnp.dot(a, wd_ref[lo:lo + size, :], preferred_element_type=_F32)
            st["acc"] = part if "acc" not in st else st["acc"] + part

        def final():
            h3 = st["h2"] + 0.5 * _rms_norm(st["acc"], post2_ref[...])
            out_ref[0, rows, :] = _rms_norm(h3, final_ref[...])

        return wout, norm, gate, up, down, final

    half_rows = n_tok // 2
    dense = [dense_half(slice(0, half_rows)), dense_half(slice(half_rows, n_tok))]

    for hist, new, pad in ((ka_all, ka_ref, PAD_A), (va_all, va_ref, PAD_A),
                           (kb_all, kb_ref, PAD_B), (vb_all, vb_ref, PAD_B)):
        hist[0:pad] = hist[n_tok:n_tok + pad]
        hist[pad:pad + n_tok] = new[0]

    row_head = lax.broadcasted_iota(jnp.int32, (4 * CHUNK, 4 * HEAD_DIM), 0) // CHUNK
    lane_head = lax.broadcasted_iota(jnp.int32, (4 * CHUNK, 4 * HEAD_DIM), 1) // HEAD_DIM
    block_diag = row_head == lane_head
    lane_head_a = lax.broadcasted_iota(jnp.int32, (CHUNK, 4 * HEAD_DIM), 1) // HEAD_DIM
    col_a = lax.broadcasted_iota(jnp.int32, (4 * CHUNK, BAND_A), 1)
    col_b = lax.broadcasted_iota(jnp.int32, (4 * CHUNK, BAND_B), 1)
    low_half = lax.broadcasted_iota(jnp.int32, (CHUNK, LANES), 1) < HEAD_DIM
    row_blk = lax.broadcasted_iota(jnp.int32, (4 * CHUNK, 1), 0) // CHUNK
    sink_cols = []
    for kv in range(N_KV_B):
        col = jnp.zeros((4 * CHUNK, 1), _F32)
        for gq in range(4):
            col = jnp.where(row_blk == gq, sinks_ref[4 * kv + gq], col)
        sink_cols.append(col)

    def attn_a(c, grp):
        r0 = c * CHUNK
        cols = slice(grp * 4 * HEAD_DIM, (grp + 1) * 4 * HEAD_DIM)
        st = {}

        def scores():
            inv = jnp.where(seq_step == 0, PAD_A - r0, 0)
            q4 = qa_ref[0, r0:r0 + CHUNK, cols]
            q_bd = jnp.where(block_diag, jnp.concatenate([q4] * 4, axis=0), jnp.zeros((), _BF16))
            k_band = ka_all[r0:r0 + BAND_A, cols]
            s = lax.dot_general(q_bd, k_band, _NT, preferred_element_type=_F32) + bias_ref[grp]
            s = jnp.where(col_a < inv, NEG_INF, s)
            p = jnp.exp(s - jnp.max(s, axis=-1, keepdims=True))
            st["p"] = (p * (1.0 / jnp.sum(p, axis=-1, keepdims=True))).astype(_BF16)

        def values():
            v_band = va_all[r0:r0 + BAND_A, cols]
            o_full = jnp.dot(st.pop("p"), v_band, preferred_element_type=_F32)
            o = jnp.zeros((CHUNK, 4 * HEAD_DIM), _F32)
            for h in range(4):
                o = jnp.where(lane_head_a == h, o_full[h * CHUNK:(h + 1) * CHUNK], o)
            o_scr[r0:r0 + CHUNK, cols] = o.astype(_BF16)
            return _zero_token(o, zero_ref[0])

        return scores, values

    def attn_b(c, kv):
        r0 = c * CHUNK
        lanes_kv = slice(kv * LANES, (kv + 1) * LANES)
        st = {}

        def scores():
            inv = jnp.where(seq_step == 0, PAD_B - r0, 0)
            k_band = kb_all[r0:r0 + BAND_B, lanes_kv]
            rows = []
            for j in (2 * kv, 2 * kv + 1):
                qc = qb_ref[0, r0:r0 + CHUNK, j * LANES:(j + 1) * LANES]
                rows.append(jnp.where(low_half, qc, jnp.zeros((), _BF16)))
                rows.append(jnp.where(low_half, jnp.zeros((), _BF16), qc))
            lhs = jnp.concatenate(rows, axis=0)
            s = lax.dot_general(lhs, k_band, _NT, preferred_element_type=_F32)
            if r0 < PAD_B:
                s = jnp.where(col_b < inv, NEG_INF, s)
            sink = sink_cols[kv]
            m = jnp.maximum(jnp.max(s, axis=-1, keepdims=True), sink)
            e = jnp.exp(s - m)
            denom = jnp.sum(e, axis=-1, keepdims=True) + jnp.exp(sink - m)
            st["e"] = (e * (1.0 / denom)).astype(_BF16)

        def values():
            v_band = vb_all[r0:r0 + BAND_B, lanes_kv]
            o_full = jnp.dot(st.pop("e"), v_band, preferred_element_type=_F32)
            for jj in range(2):
                o = jnp.where(low_half, o_full[(2 * jj) * CHUNK:(2 * jj + 1) * CHUNK],
                              o_full[(2 * jj + 1) * CHUNK:(2 * jj + 2) * CHUNK])
                j = 2 * kv + jj
                o_scr[r0:r0 + CHUNK, W_A + j * LANES:W_A + (j + 1) * LANES] = o.astype(_BF16)
            return _zero_token(o, zero_ref[0])

        return scores, values

    attn = []
    for c in range(n_tok // CHUNK):
        attn += [attn_a(c, 0), attn_a(c, 1), attn_b(c, 0), attn_b(c, 1)]

    tokens = []
    started = [0]

    def attention_slot():
        j = started[0]
        if j < len(attn):
            attn[j][0]()
        if 0 <= j - 2 < len(attn):
            tokens.append(attn[j - 2][1]())
        started[0] = j + 1

    def take_token():
        token = None
        while len(tokens) > 1:
            zero = tokens.pop(0)
            token = zero if token is None else token + zero
        return token

    dense[0][0]()
    attention_slot()
    dense[1][0]()
    attention_slot()
    dense[0][1]()
    attention_slot()
    first_lo, last_lo = FF_CHUNKS[0][0], FF_CHUNKS[-1][0]
    for lo, size in FF_CHUNKS:
        for half, (_, norm, gate, up, _, _) in enumerate(dense):
            if lo == first_lo and half == 1:
                norm()
                attention_slot()
            gate(lo, size)
            attention_slot()
            up(lo, size)
            attention_slot()
        for _, _, _, _, down, final in dense:
            down(lo, size, take_token())
            attention_slot()
            if lo == last_lo:
                final()
    assert started[0] > len(attn) + 1


def _bias_call(rel):
    return pl.pallas_call(
        _bias_kernel,
        out_shape=jax.ShapeDtypeStruct((2, 4 * CHUNK, BAND_A), _F32),
        name="rel_bias_table",
    )(rel)


def _resident(shape):
    return pl.BlockSpec(shape, lambda *_: (0,) * len(shape), pipeline_mode=pl.Buffered(1))


def _ffn_proj_call(x, pos, freq, pre1, wg, wu, wd, post1, prem, win):
    batch, seq, d = x.shape
    t = TOKENS_PER_STEP
    per_seq = seq // t
    n_tiles = batch * per_seq

    def cur(w):
        def index(g):
            tile = jnp.minimum(g, n_tiles - 1)
            return (tile // per_seq, tile % per_seq, 0)
        return pl.BlockSpec((1, t, w), index)

    def prev(w):
        def index(g):
            tile = jnp.maximum(g - 1, 0)
            return (tile // per_seq, tile % per_seq, 0)
        return pl.BlockSpec((1, t, w), index)

    prev_pos = pl.BlockSpec((1, 1, t), lambda g: (jnp.maximum(g - 1, 0), 0, 0))
    run_time_zero = jnp.zeros((1,), jnp.int32)
    out_shapes = [jax.ShapeDtypeStruct((batch, seq, d), _F32)] + [
        jax.ShapeDtypeStruct((batch, seq, w), _BF16) for w in (W_A, W_A, W_A, W_QB, W_KVB_DUP, W_KVB_DUP)]
    return pl.pallas_call(
        _ffn_proj_kernel,
        grid=(n_tiles + 1,),
        in_specs=[pl.BlockSpec(memory_space=pltpu.SMEM), cur(d), prev_pos, _resident((ROT_DIM // 2, 1)),
                  _resident((1, d)),
                  _resident((d, D_FF)), _resident((d, D_FF)), _resident((D_FF, d)),
                  _resident((1, d)), _resident((1, d)), _resident((d, D_IN))],
        out_specs=[cur(d), prev(W_A), prev(W_A), prev(W_A), prev(W_QB), prev(W_KVB_DUP), prev(W_KVB_DUP)],
        out_shape=out_shapes,
        scratch_shapes=[pltpu.VMEM((t, d), _BF16)],
        compiler_params=pltpu.CompilerParams(dimension_semantics=("arbitrary",),
                                             vmem_limit_bytes=VMEM_LIMIT_BYTES),
        name="ffn1_proj",
    )(run_time_zero, x, pos, freq, pre1, wg, wu, wd, post1, prem, win)


def _mixer_ffn_call(sinks, h1, qa, ka, va, qb, kb, vb, bias, wout, postm, pre2, wg, wu, wd, post2, final):
    batch, seq, d = h1.shape
    t = TOKENS_PER_STEP
    per_seq = seq // t
    n_tiles = batch * per_seq

    def att(w):
        def index(g):
            tile = jnp.minimum(g, n_tiles - 1)
            return (tile // per_seq, tile % per_seq, 0)
        return pl.BlockSpec((1, t, w), index)

    def tail(w):
        def index(g):
            tile = jnp.maximum(g - 1, 0)
            return (tile // per_seq, tile % per_seq, 0)
        return pl.BlockSpec((1, t, w), index)

    smem = pl.BlockSpec(memory_space=pltpu.SMEM)
    run_time_zero = jnp.zeros((1,), jnp.int32)
    return pl.pallas_call(
        _mixer_ffn_kernel,
        grid=(n_tiles + 1,),
        in_specs=[smem, smem,
                  tail(d), att(W_A), att(W_QB), att(W_A), att(W_A), att(W_KVB_DUP), att(W_KVB_DUP),
                  _resident((2, 4 * CHUNK, BAND_A)),
                  _resident((d, d)), _resident((1, d)), _resident((1, d)),
                  _resident((d, D_FF)), _resident((d, D_FF)), _resident((D_FF, d)),
                  _resident((1, d)), _resident((1, d))],
        out_specs=tail(d),
        out_shape=jax.ShapeDtypeStruct((batch, seq, d), _F32),
        scratch_shapes=[pltpu.VMEM((PAD_A + t, W_A), _BF16), pltpu.VMEM((PAD_A + t, W_A), _BF16),
                        pltpu.VMEM((PAD_B + t, W_KVB_DUP), _BF16), pltpu.VMEM((PAD_B + t, W_KVB_DUP), _BF16),
                        pltpu.VMEM((t, 2 * W_A), _BF16)],
        compiler_params=pltpu.CompilerParams(dimension_semantics=("arbitrary",),
                                             vmem_limit_bytes=VMEM_LIMIT_BYTES),
        name="mixer_ffn2",
    )(sinks, run_time_zero, h1, qa, qb, ka, va, kb, vb, bias, wout, postm, pre2, wg, wu, wd, post2, final)


def kernel(x, positions, ffn1_pre_g, ffn1_w_gate, ffn1_w_up, ffn1_w_down, ffn1_post_g, mix_pre_g, w_in, rel_bias_a, sinks_b, w_out, mix_post_g, ffn2_pre_g, ffn2_w_gate, ffn2_w_up, ffn2_w_down, ffn2_post_g, final_g):
    assert x.shape[1] == SEQ and x.shape[2] == D_MODEL and ffn1_pre_g.shape[0] == 1
    row = lambda g: g[0].reshape(1, D_MODEL).astype(_F32)
    bf = lambda w: w[0].astype(_BF16)

    half = ROT_DIM // 2
    freq = jnp.power(jnp.float32(ROPE_THETA), -jnp.arange(half, dtype=_F32) * (2.0 / ROT_DIM)).reshape(half, 1)
    pos_rows = positions.reshape(-1, 1, TOKENS_PER_STEP)

    rel_pad = jnp.pad(rel_bias_a[0].astype(_F32), ((0, 0), (0, REL_PAD - (2 * MAX_REL + 1))))

    h1, qa, ka, va, qb, kb, vb = _ffn_proj_call(
        x, pos_rows, freq, row(ffn1_pre_g), bf(ffn1_w_gate), bf(ffn1_w_up), bf(ffn1_w_down),
        row(ffn1_post_g), row(mix_pre_g), bf(w_in))
    return _mixer_ffn_call(
        sinks_b[0].astype(_F32), h1, qa, ka, va, qb, kb, vb, _bias_call(rel_pad), bf(w_out),
        row(mix_post_g), row(ffn2_pre_g), bf(ffn2_w_gate), bf(ffn2_w_up), bf(ffn2_w_down),
        row(ffn2_post_g), row(final_g))
```

```python
import jax
import jax.numpy as jnp
from jax import lax
from jax.experimental import pallas as pl
from jax.experimental.pallas import tpu as pltpu

D_MODEL = 1024
SEQ = 2048
CHUNK = 64
HEAD_DIM = 64
N_HEADS_A = 8
LEFT_CHUNKS_A = 8
MAX_REL = 128
N_HEADS_B = 8
N_KV_B = 2
LEFT_CHUNKS_B = 2
ROT_DIM = HEAD_DIM // 4
ROPE_THETA = 500000.0
D_FF = 2816
EPS = 1e-6
NEG_INF = -1e30

W_A = N_HEADS_A * HEAD_DIM
W_QB = N_HEADS_B * HEAD_DIM
W_KVB = N_KV_B * HEAD_DIM
W_KVB_DUP = 2 * W_KVB
D_IN = 3 * W_A + W_QB + 2 * W_KVB
BAND_A = (LEFT_CHUNKS_A + 1) * CHUNK
BAND_B = (LEFT_CHUNKS_B + 1) * CHUNK
PAD_A = LEFT_CHUNKS_A * CHUNK
PAD_B = LEFT_CHUNKS_B * CHUNK
SCALE = HEAD_DIM ** -0.5
LOG2E = 1.4426950408889634

LANES = 128
TOEPLITZ_W = 768
REL_PAD = 384

TOKENS_PER_STEP = 512
FF_CHUNKS = tuple((lo, min(512, D_FF - lo)) for lo in range(0, D_FF, 512))
VMEM_LIMIT_BYTES = 63 * 1024 * 1024

_F32 = jnp.float32
_BF16 = jnp.bfloat16
_NT = (((1,), (1,)), ((), ()))


def _rms_norm(x, g):
    return x * lax.rsqrt(jnp.mean(x * x, axis=-1, keepdims=True) + EPS) * g


def _rope_cols(x, cos, sin_lo, sin_hi):
    return x * cos + pltpu.roll(x, ROT_DIM // 2, 1) * sin_hi + pltpu.roll(x, LANES - ROT_DIM // 2, 1) * sin_lo


def _zero_token(x, zero):
    return pltpu.bitcast(pltpu.bitcast(x[0:8, 0:LANES], jnp.int32) & zero, _F32)


def _after(x, token):
    if token is None:
        return x
    top = x[0:8, :] + jnp.tile(token, (1, x.shape[1] // LANES))
    return jnp.concatenate([top, x[8:, :]], axis=0)


def _ffn_proj_kernel(zero_ref, x_ref, pos_ref, freq_ref, pre1_ref, wg_ref, wu_ref, wd_ref, post1_ref, prem_ref,
                     win_ref, h1_ref, qa_ref, ka_ref, va_ref, qb_ref, kb_ref, vb_ref, u_scr):
    n_tok = TOKENS_PER_STEP
    half_rows = n_tok // 2

    @pl.when(pl.program_id(0) == 0)
    def _():
        u_scr[...] = jnp.zeros(u_scr.shape, u_scr.dtype)

    def proj_half(rows):
        st = {}

        def tables():
            n = rows.stop - rows.start
            ang = freq_ref[...] * pos_ref[0][:, rows].astype(_F32)
            packed = jnp.concatenate([jnp.cos(ang), jnp.sin(ang), jnp.zeros((LANES - ROT_DIM, n), _F32)], axis=0)
            cs = packed.T
            lane = lax.broadcasted_iota(jnp.int32, cs.shape, 1)
            in_head = lane % HEAD_DIM
            first_head = lane < HEAD_DIM
            half = ROT_DIM // 2

            def both_heads(t):
                return jnp.where(first_head, t, pltpu.roll(t, HEAD_DIM, 1))

            cos = both_heads(jnp.where(lane < half, cs, pltpu.roll(cs, half, 1)))
            sin_lo = both_heads(pltpu.roll(cs, LANES - half, 1))
            sin_hi = both_heads(cs)
            st["cos"] = jnp.where(in_head < ROT_DIM, cos, 1.0)
            st["sin_lo"] = jnp.where(in_head < half, -sin_lo, 0.0)
            st["sin_hi"] = jnp.where((in_head >= half) & (in_head < ROT_DIM), sin_hi, 0.0)
            st["low"] = first_head
            return _zero_token(st["cos"] + st["sin_hi"], zero_ref[0])

        def cols(c0, width):
            return jnp.dot(u_scr[rows, :], win_ref[:, c0:c0 + width], preferred_element_type=_F32)

        def rope(blk):
            return _rope_cols(blk, st["cos"], st["sin_lo"], st["sin_hi"])

        def twice(blk):
            swapped = pltpu.roll(blk, HEAD_DIM, 1)
            return jnp.concatenate([jnp.where(st["low"], blk, swapped), jnp.where(st["low"], swapped, blk)], axis=1)

        def qa():
            q = cols(0, W_A) * (SCALE * LOG2E)
            qa_ref[0, rows, :] = q.astype(_BF16)
            return _zero_token(q, zero_ref[0])

        def ka():
            k = cols(W_A, W_A)
            ka_ref[0, rows, :] = k.astype(_BF16)
            return _zero_token(k, zero_ref[0])

        def va():
            v = cols(2 * W_A, W_A)
            va_ref[0, rows, :] = v.astype(_BF16)
            return _zero_token(v, zero_ref[0])

        def qb():
            p = cols(3 * W_A, W_QB)
            for j in range(W_QB // LANES):
                q = rope(p[:, j * LANES:(j + 1) * LANES]) * (SCALE * LOG2E)
                qb_ref[0, rows, j * LANES:(j + 1) * LANES] = q.astype(_BF16)
            return _zero_token(q, zero_ref[0])

        def kvb():
            p = cols(3 * W_A + W_QB, 2 * W_KVB)
            k = twice(rope(p[:, 0:W_KVB]))
            kb_ref[0, rows, :] = k.astype(_BF16)
            vb_ref[0, rows, :] = twice(p[:, W_KVB:2 * W_KVB]).astype(_BF16)
            return _zero_token(k, zero_ref[0])

        return tables, qa, ka, va, qb, kvb

    def ffn_half(rows):
        st = {}

        def norm():
            st["x"] = x_ref[0, rows, :]
            st["xn"] = _rms_norm(st["x"], pre1_ref[...]).astype(_BF16)

        def gate(lo, size):
            st["g"] = jnp.dot(st["xn"], wg_ref[:, lo:lo + size], preferred_element_type=_F32)

        def up(lo, size):
            st["u"] = jnp.dot(st["xn"], wu_ref[:, lo:lo + size], preferred_element_type=_F32)

        def down(lo, size, token):
            gt = _after(st.pop("g"), token)
            a = (gt * (1.0 / (1.0 + jnp.exp2(gt * -LOG2E))) * st.pop("u")).astype(_BF16)
            part = jnp.dot(a, wd_ref[lo:lo + size, :], preferred_element_type=_F32)
            st["acc"] = part if "acc" not in st else st["acc"] + part

        def final():
            h1 = st["x"] + 0.5 * _rms_norm(st["acc"], post1_ref[...])
            h1_ref[0, rows, :] = h1
            st["u_next"] = _rms_norm(h1, prem_ref[...]).astype(_BF16)

        def carry():
            u_scr[rows, :] = st.pop("u_next")

        return norm, gate, up, down, final, carry

    halves = (slice(0, half_rows), slice(half_rows, n_tok))
    ffn = [ffn_half(rows) for rows in halves]
    tables0, qa0, ka0, va0, qb0, kvb0 = proj_half(halves[0])
    tables1, qa1, ka1, va1, qb1, kvb1 = proj_half(halves[1])
    proj_pieces = [qa0, ka0, va0, qa1, qb0, kvb0, qb1, kvb1, ka1, va1]
    held_back = 2
    tokens = []

    def proj_slot():
        if len(proj_pieces) > held_back:
            tokens.append(proj_pieces.pop(0)())

    def take_token(lag=2):
        token = None
        while len(tokens) > lag:
            zero = tokens.pop(0)
            token = zero if token is None else token + zero
        return token

    ffn[0][0]()
    proj_slot()
    ffn[1][0]()
    proj_slot()
    tokens.append(tables0())
    proj_slot()
    tokens.append(tables1())
    last_lo = FF_CHUNKS[-1][0]
    for lo, size in FF_CHUNKS:
        for _, gate, up, _, _, _ in ffn:
            gate(lo, size)
            up(lo, size)
        for _, _, _, down, final, _ in ffn:
            down(lo, size, take_token())
            proj_slot()
            if lo == last_lo:
                final()
    while proj_pieces:
        proj_pieces.pop(0)()
    for *_, carry in ffn:
        carry()


def _bias_kernel(rel_ref, bias_ref):
    m = lax.broadcasted_iota(jnp.int32, (REL_PAD, TOEPLITZ_W), 1)
    r = lax.broadcasted_iota(jnp.int32, (REL_PAD, TOEPLITZ_W), 0)
    sel = (r == jnp.clip(PAD_A + CHUNK - 1 - m, -MAX_REL, MAX_REL) + MAX_REL).astype(_F32)
    g = jnp.dot(rel_ref[...], sel, preferred_element_type=_F32, precision=lax.Precision.HIGHEST)
    for h in range(N_HEADS_A):
        rows = jnp.broadcast_to(g[h:h + 1, :], (CHUNK, TOEPLITZ_W))
        rolled = pltpu.roll(rows, TOEPLITZ_W - (CHUNK - 1), 1, stride=1, stride_axis=0)
        bias_ref[h // 4, (h % 4) * CHUNK:(h % 4 + 1) * CHUNK, :] = rolled[:, :BAND_A] * LOG2E


def _mixer_ffn_kernel(sinks_ref, zero_ref, h1_ref, qa_ref, qb_ref, ka_ref, va_ref, kb_ref, vb_ref, bias_ref,
                      wout_ref, postm_ref, pre2_ref, wg_ref, wu_ref, wd_ref, post2_ref, final_ref,
                      out_ref, ka_all, va_all, kb_all, vb_all, o_scr):
    g = pl.program_id(0)
    n_tok = TOKENS_PER_STEP
    steps_per_seq = SEQ // n_tok
    seq_step = jnp.minimum(g, pl.num_programs(0) - 2) % steps_per_seq

    @pl.when(g == 0)
    def _():
        for ref in (o_scr, ka_all, va_all, kb_all, vb_all):
            ref[...] = jnp.zeros(ref.shape, ref.dtype)

    def dense_half(rows):
        st = {}

        def wout():
            st["mixed"] = jnp.dot(o_scr[rows, :], wout_ref[...], preferred_element_type=_F32)

        def norm():
            st["h2"] = h1_ref[0, rows, :] + _rms_norm(st.pop("mixed"), postm_ref[...])
            st["xn"] = _rms_norm(st["h2"], pre2_ref[...]).astype(_BF16)

        def gate(lo, size):
            st["g"] = jnp.dot(st["xn"], wg_ref[:, lo:lo + size], preferred_element_type=_F32)

        def up(lo, size):
            st["u"] = jnp.dot(st["xn"], wu_ref[:, lo:lo + size], preferred_element_type=_F32)

        def down(lo, size, token):
            gt = _after(st.pop("g"), token)
            a = (gt * (1.0 / (1.0 + jnp.exp2(gt * -LOG2E))) * st.pop("u")).astype(_BF16)
            part = jnp.dot(a, wd_ref[lo:lo + size, :], preferred_element_type=_F32)
            st["acc"] = part if "acc" not in st else st["acc"] + part

        def final():
            h3 = st["h2"] + 0.5 * _rms_norm(st["acc"], post2_ref[...])
            out_ref[0, rows, :] = _rms_norm(h3, final_ref[...])

        return wout, norm, gate, up, down, final

    half_rows = n_tok // 2
    dense = [dense_half(slice(0, half_rows)), dense_half(slice(half_rows, n_tok))]

    for hist, new, pad in ((ka_all, ka_ref, PAD_A), (va_all, va_ref, PAD_A),
                           (kb_all, kb_ref, PAD_B), (vb_all, vb_ref, PAD_B)):
        hist[0:pad] = hist[n_tok:n_tok + pad]
        hist[pad:pad + n_tok] = new[0]

    row_head = lax.broadcasted_iota(jnp.int32, (4 * CHUNK, 4 * HEAD_DIM), 0) // CHUNK
    lane_head = lax.broadcasted_iota(jnp.int32, (4 * CHUNK, 4 * HEAD_DIM), 1) // HEAD_DIM
    block_diag = row_head == lane_head
    lane_head_a = lax.broadcasted_iota(jnp.int32, (CHUNK, 4 * HEAD_DIM), 1) // HEAD_DIM
    col_a = lax.broadcasted_iota(jnp.int32, (1, BAND_A), 1)
    col_b = lax.broadcasted_iota(jnp.int32, (1, BAND_B), 1)
    low_half = lax.broadcasted_iota(jnp.int32, (CHUNK, LANES), 1) < HEAD_DIM
    row_blk = lax.broadcasted_iota(jnp.int32, (4 * CHUNK, 1), 0) // CHUNK
    sink_cols = []
    for kv in range(N_KV_B):
        col = jnp.zeros((4 * CHUNK, 1), _F32)
        for gq in range(4):
            col = jnp.where(row_blk == gq, sinks_ref[4 * kv + gq] * LOG2E, col)
        sink_cols.append(col)

    def attn_a(c, grp):
        r0 = c * CHUNK
        cols = slice(grp * 4 * HEAD_DIM, (grp + 1) * 4 * HEAD_DIM)
        st = {}

        def scores():
            inv = jnp.where(seq_step == 0, PAD_A - r0, 0)
            q4 = qa_ref[0, r0:r0 + CHUNK, cols]
            q_bd = jnp.where(block_diag, jnp.concatenate([q4] * 4, axis=0), jnp.zeros((), _BF16))
            k_band = ka_all[r0:r0 + BAND_A, cols]
            s = lax.dot_general(q_bd, k_band, _NT, preferred_element_type=_F32) + bias_ref[grp]
            s = s + jnp.where(col_a < inv, NEG_INF, 0.0)
            p = jnp.exp2(s - jnp.max(s, axis=-1, keepdims=True))
            st["p"] = (p * (1.0 / jnp.sum(p, axis=-1, keepdims=True))).astype(_BF16)

        def values():
            v_band = va_all[r0:r0 + BAND_A, cols]
            o_full = jnp.dot(st.pop("p"), v_band, preferred_element_type=_F32)
            o = jnp.zeros((CHUNK, 4 * HEAD_DIM), _F32)
            for h in range(4):
                o = jnp.where(lane_head_a == h, o_full[h * CHUNK:(h + 1) * CHUNK], o)
            o_scr[r0:r0 + CHUNK, cols] = o.astype(_BF16)
            return _zero_token(o, zero_ref[0])

        return scores, values

    def attn_b(c, kv):
        r0 = c * CHUNK
        lanes_kv = slice(kv * LANES, (kv + 1) * LANES)
        st = {}

        def scores():
            inv = jnp.where(seq_step == 0, PAD_B - r0, 0)
            k_band = kb_all[r0:r0 + BAND_B, lanes_kv]
            rows = []
            for j in (2 * kv, 2 * kv + 1):
                qc = qb_ref[0, r0:r0 + CHUNK, j * LANES:(j + 1) * LANES]
                rows.append(jnp.where(low_half, qc, jnp.zeros((), _BF16)))
                rows.append(jnp.where(low_half, jnp.zeros((), _BF16), qc))
            lhs = jnp.concatenate(rows, axis=0)
            s = lax.dot_general(lhs, k_band, _NT, preferred_element_type=_F32)
            if r0 < PAD_B:
                s = s + jnp.where(col_b < inv, NEG_INF, 0.0)
            sink = sink_cols[kv]
            m = jnp.maximum(jnp.max(s, axis=-1, keepdims=True), sink)
            e = jnp.exp2(s - m)
            denom = jnp.sum(e, axis=-1, keepdims=True) + jnp.exp2(sink - m)
            st["e"] = (e * (1.0 / denom)).astype(_BF16)

        def values():
            v_band = vb_all[r0:r0 + BAND_B, lanes_kv]
            o_full = jnp.dot(st.pop("e"), v_band, preferred_element_type=_F32)
            for jj in range(2):
                o = jnp.where(low_half, o_full[(2 * jj) * CHUNK:(2 * jj + 1) * CHUNK],
                              o_full[(2 * jj + 1) * CHUNK:(2 * jj + 2) * CHUNK])
                j = 2 * kv + jj
                o_scr[r0:r0 + CHUNK, W_A + j * LANES:W_A + (j + 1) * LANES] = o.astype(_BF16)
            return _zero_token(o, zero_ref[0])

        return scores, values

    attn = []
    for c in range(n_tok // CHUNK):
        attn += [attn_a(c, 0), attn_a(c, 1), attn_b(c, 0), attn_b(c, 1)]

    tokens = []
    started = [0]

    def attention_slot():
        j = started[0]
        if j < len(attn):
            attn[j][0]()
        if 0 <= j - 2 < len(attn):
            tokens.append(attn[j - 2][1]())
        started[0] = j + 1

    def take_token():
        token = None
        while len(tokens) > 1:
            zero = tokens.pop(0)
            token = zero if token is None else token + zero
        return token

    dense[0][0]()
    attention_slot()
    dense[1][0]()
    attention_slot()
    dense[0][1]()
    attention_slot()
    first_lo, last_lo = FF_CHUNKS[0][0], FF_CHUNKS[-1][0]
    for lo, size in FF_CHUNKS:
        for half, (_, norm, gate, up, _, _) in enumerate(dense):
            if lo == first_lo and half == 1:
                norm()
                attention_slot()
            gate(lo, size)
            attention_slot()
            up(lo, size)
            attention_slot()
        for _, _, _, _, down, final in dense:
            down(lo, size, take_token())
            attention_slot()
            if lo == last_lo:
                final()
    assert started[0] > len(attn) + 1


def _bias_call(rel):
    return pl.pallas_call(
        _bias_kernel,
        out_shape=jax.ShapeDtypeStruct((2, 4 * CHUNK, BAND_A), _F32),
        name="rel_bias_table",
    )(rel)


def _resident(shape):
    return pl.BlockSpec(shape, lambda *_: (0,) * len(shape), pipeline_mode=pl.Buffered(1))


def _ffn_proj_call(x, pos, freq, pre1, wg, wu, wd, post1, prem, win):
    batch, seq, d = x.shape
    t = TOKENS_PER_STEP
    per_seq = seq // t
    n_tiles = batch * per_seq

    def cur(w):
        def index(g):
            tile = jnp.minimum(g, n_tiles - 1)
            return (tile // per_seq, tile % per_seq, 0)
        return pl.BlockSpec((1, t, w), index)

    def prev(w):
        def index(g):
            tile = jnp.maximum(g - 1, 0)
            return (tile // per_seq, tile % per_seq, 0)
        return pl.BlockSpec((1, t, w), index)

    prev_pos = pl.BlockSpec((1, 1, t), lambda g: (jnp.maximum(g - 1, 0), 0, 0))
    run_time_zero = jnp.zeros((1,), jnp.int32)
    out_shapes = [jax.ShapeDtypeStruct((batch, seq, d), _F32)] + [
        jax.ShapeDtypeStruct((batch, seq, w), _BF16) for w in (W_A, W_A, W_A, W_QB, W_KVB_DUP, W_KVB_DUP)]
    return pl.pallas_call(
        _ffn_proj_kernel,
        grid=(n_tiles + 1,),
        in_specs=[pl.BlockSpec(memory_space=pltpu.SMEM), cur(d), prev_pos, _resident((ROT_DIM // 2, 1)),
                  _resident((1, d)),
                  _resident((d, D_FF)), _resident((d, D_FF)), _resident((D_FF, d)),
                  _resident((1, d)), _resident((1, d)), _resident((d, D_IN))],
        out_specs=[cur(d), prev(W_A), prev(W_A), prev(W_A), prev(W_QB), prev(W_KVB_DUP), prev(W_KVB_DUP)],
        out_shape=out_shapes,
        scratch_shapes=[pltpu.VMEM((t, d), _BF16)],
        compiler_params=pltpu.CompilerParams(dimension_semantics=("arbitrary",),
                                             vmem_limit_bytes=VMEM_LIMIT_BYTES),
        name="ffn1_proj",
    )(run_time_zero, x, pos, freq, pre1, wg, wu, wd, post1, prem, win)


def _mixer_ffn_call(sinks, h1, qa, ka, va, qb, kb, vb, bias, wout, postm, pre2, wg, wu, wd, post2, final):
    batch, seq, d = h1.shape
    t = TOKENS_PER_STEP
    per_seq = seq // t
    n_tiles = batch * per_seq

    def att(w):
        def index(g):
            tile = jnp.minimum(g, n_tiles - 1)
            return (tile // per_seq, tile % per_seq, 0)
        return pl.BlockSpec((1, t, w), index)

    def tail(w):
        def index(g):
            tile = jnp.maximum(g - 1, 0)
            return (tile // per_seq, tile % per_seq, 0)
        return pl.BlockSpec((1, t, w), index)

    smem = pl.BlockSpec(memory_space=pltpu.SMEM)
    run_time_zero = jnp.zeros((1,), jnp.int32)
    return pl.pallas_call(
        _mixer_ffn_kernel,
        grid=(n_tiles + 1,),
        in_specs=[smem, smem,
                  tail(d), att(W_A), att(W_QB), att(W_A), att(W_A), att(W_KVB_DUP), att(W_KVB_DUP),
                  _resident((2, 4 * CHUNK, BAND_A)),
                  _resident((d, d)), _resident((1, d)), _resident((1, d)),
                  _resident((d, D_FF)), _resident((d, D_FF)), _resident((D_FF, d)),
                  _resident((1, d)), _resident((1, d))],
        out_specs=tail(d),
        out_shape=jax.ShapeDtypeStruct((batch, seq, d), _F32),
        scratch_shapes=[pltpu.VMEM((PAD_A + t, W_A), _BF16), pltpu.VMEM((PAD_A + t, W_A), _BF16),
                        pltpu.VMEM((PAD_B + t, W_KVB_DUP), _BF16), pltpu.VMEM((PAD_B + t, W_KVB_DUP), _BF16),
                        pltpu.VMEM((t, 2 * W_A), _BF16)],
        compiler_params=pltpu.CompilerParams(dimension_semantics=("arbitrary",),
                                             vmem_limit_bytes=VMEM_LIMIT_BYTES),
        name="mixer_ffn2",
    )(sinks, run_time_zero, h1, qa, qb, ka, va, kb, vb, bias, wout, postm, pre2, wg, wu, wd, post2, final)


def kernel(x, positions, ffn1_pre_g, ffn1_w_gate, ffn1_w_up, ffn1_w_down, ffn1_post_g, mix_pre_g, w_in, rel_bias_a, sinks_b, w_out, mix_post_g, ffn2_pre_g, ffn2_w_gate, ffn2_w_up, ffn2_w_down, ffn2_post_g, final_g):
    assert x.shape[1] == SEQ and x.shape[2] == D_MODEL and ffn1_pre_g.shape[0] == 1
    row = lambda g: g[0].reshape(1, D_MODEL).astype(_F32)
    bf = lambda w: w[0].astype(_BF16)

    half = ROT_DIM // 2
    freq = jnp.power(jnp.float32(ROPE_THETA), -jnp.arange(half, dtype=_F32) * (2.0 / ROT_DIM)).reshape(half, 1)
    pos_rows = positions.reshape(-1, 1, TOKENS_PER_STEP)

    rel_pad = jnp.pad(rel_bias_a[0].astype(_F32), ((0, 0), (0, REL_PAD - (2 * MAX_REL + 1))))

    h1, qa, ka, va, qb, kb, vb = _ffn_proj_call(
        x, pos_rows, freq, row(ffn1_pre_g), bf(ffn1_w_gate), bf(ffn1_w_up), bf(ffn1_w_down),
        row(ffn1_post_g), row(mix_pre_g), bf(w_in))
    return _mixer_ffn_call(
        sinks_b[0].astype(_F32), h1, qa, ka, va, qb, kb, vb, _bias_call(rel_pad), bf(w_out),
        row(mix_post_g), row(ffn2_pre_g), bf(ffn2_w_gate), bf(ffn2_w_up), bf(ffn2_w_down),
        row(ffn2_post_g), row(final_g))
```

```python
import jax
import jax.numpy as jnp
from jax import lax
from jax.experimental import pallas as pl
from jax.experimental.pallas import tpu as pltpu

D_MODEL = 1024
SEQ = 2048
CHUNK = 64
HEAD_DIM = 64
N_HEADS_A = 8
LEFT_CHUNKS_A = 8
MAX_REL = 128
N_HEADS_B = 8
N_KV_B = 2
LEFT_CHUNKS_B = 2
ROT_DIM = HEAD_DIM // 4
ROPE_THETA = 500000.0
D_FF = 2816
EPS = 1e-6
NEG_INF = -1e30

W_A = N_HEADS_A * HEAD_DIM
W_QB = N_HEADS_B * HEAD_DIM
W_KVB = N_KV_B * HEAD_DIM
W_KVB_DUP = 2 * W_KVB
D_IN = 3 * W_A + W_QB + 2 * W_KVB
BAND_A = (LEFT_CHUNKS_A + 1) * CHUNK
BAND_B = (LEFT_CHUNKS_B + 1) * CHUNK
PAD_A = LEFT_CHUNKS_A * CHUNK
PAD_B = LEFT_CHUNKS_B * CHUNK
SCALE = HEAD_DIM ** -0.5
LOG2E = 1.4426950408889634

LANES = 128
SUBLANES = 8
TOEPLITZ_W = 768
REL_PAD = 384

TOKENS_PER_STEP = 512
FF_CHUNKS = tuple((lo, min(512, D_FF - lo)) for lo in range(0, D_FF, 512))
VMEM_LIMIT_BYTES = 63 * 1024 * 1024

_F32 = jnp.float32
_BF16 = jnp.bfloat16
_NT = (((1,), (1,)), ((), ()))


def _rms_norm(x, g):
    return x * lax.rsqrt(jnp.mean(x * x, axis=-1, keepdims=True) + EPS) * g


def _rope_cols(x, cos, sin_lo, sin_hi):
    return x * cos + pltpu.roll(x, ROT_DIM // 2, 1) * sin_hi + pltpu.roll(x, LANES - ROT_DIM // 2, 1) * sin_lo


def _zero_token(x, zero):
    return pltpu.bitcast(pltpu.bitcast(x[0:SUBLANES, 0:LANES], jnp.int32) & zero, _F32)


def _after(x, token):
    if token is None:
        return x
    top = x[0:SUBLANES, :] + jnp.tile(token, (1, x.shape[1] // LANES))
    return jnp.concatenate([top, x[SUBLANES:, :]], axis=0)


def _ffn_proj_kernel(zero_ref, x_ref, pos_ref, freq_ref, pre1_ref, wg_ref, wu_ref, wd_ref, post1_ref, prem_ref,
                     win_ref, h1_ref, qa_ref, ka_ref, va_ref, qb_ref, kb_ref, vb_ref, u_scr):
    n_tok = TOKENS_PER_STEP
    half_rows = n_tok // 2

    @pl.when(pl.program_id(0) == 0)
    def _():
        u_scr[...] = jnp.zeros(u_scr.shape, u_scr.dtype)

    def proj_half(rows):
        st = {}

        def tables():
            n = rows.stop - rows.start
            ang = freq_ref[...] * pos_ref[0][:, rows].astype(_F32)
            packed = jnp.concatenate([jnp.cos(ang), jnp.sin(ang), jnp.zeros((LANES - ROT_DIM, n), _F32)], axis=0)
            cs = packed.T
            lane = lax.broadcasted_iota(jnp.int32, cs.shape, 1)
            in_head = lane % HEAD_DIM
            first_head = lane < HEAD_DIM
            half = ROT_DIM // 2

            def both_heads(t):
                return jnp.where(first_head, t, pltpu.roll(t, HEAD_DIM, 1))

            cos = both_heads(jnp.where(lane < half, cs, pltpu.roll(cs, half, 1)))
            sin_lo = both_heads(pltpu.roll(cs, LANES - half, 1))
            sin_hi = both_heads(cs)
            st["cos"] = jnp.where(in_head < ROT_DIM, cos, 1.0)
            st["sin_lo"] = jnp.where(in_head < half, -sin_lo, 0.0)
            st["sin_hi"] = jnp.where((in_head >= half) & (in_head < ROT_DIM), sin_hi, 0.0)
            st["low"] = first_head
            return _zero_token(st["cos"] + st["sin_hi"], zero_ref[0])

        def cols(c0, width):
            return jnp.dot(u_scr[rows, :], win_ref[:, c0:c0 + width], preferred_element_type=_F32)

        def rope(blk):
            return _rope_cols(blk, st["cos"], st["sin_lo"], st["sin_hi"])

        def twice(blk):
            swapped = pltpu.roll(blk, HEAD_DIM, 1)
            return jnp.concatenate([jnp.where(st["low"], blk, swapped), jnp.where(st["low"], swapped, blk)], axis=1)

        def qa():
            q = cols(0, W_A) * (SCALE * LOG2E)
            qa_ref[0, rows, :] = q.astype(_BF16)
            return _zero_token(q, zero_ref[0])

        def ka():
            k = cols(W_A, W_A)
            ka_ref[0, rows, :] = k.astype(_BF16)
            return _zero_token(k, zero_ref[0])

        def va():
            v = cols(2 * W_A, W_A)
            va_ref[0, rows, :] = v.astype(_BF16)
            return _zero_token(v, zero_ref[0])

        def qb():
            p = cols(3 * W_A, W_QB)
            for j in range(W_QB // LANES):
                q = rope(p[:, j * LANES:(j + 1) * LANES]) * (SCALE * LOG2E)
                qb_ref[0, rows, j * LANES:(j + 1) * LANES] = q.astype(_BF16)
            return _zero_token(q, zero_ref[0])

        def kvb():
            p = cols(3 * W_A + W_QB, 2 * W_KVB)
            k = twice(rope(p[:, 0:W_KVB]))
            kb_ref[0, rows, :] = k.astype(_BF16)
            vb_ref[0, rows, :] = twice(p[:, W_KVB:2 * W_KVB]).astype(_BF16)
            return _zero_token(k, zero_ref[0])

        return tables, qa, ka, va, qb, kvb

    def ffn_half(rows):
        st = {}

        def norm():
            st["x"] = x_ref[0, rows, :]
            st["xn"] = _rms_norm(st["x"], pre1_ref[...]).astype(_BF16)

        def gate(lo, size):
            st["g"] = jnp.dot(st["xn"], wg_ref[:, lo:lo + size], preferred_element_type=_F32)

        def up(lo, size):
            st["u"] = jnp.dot(st["xn"], wu_ref[:, lo:lo + size], preferred_element_type=_F32)

        def down(lo, size, token):
            gt = _after(st.pop("g"), token)
            a = (gt * (1.0 / (1.0 + jnp.exp2(gt * -LOG2E))) * st.pop("u")).astype(_BF16)
            part = jnp.dot(a, wd_ref[lo:lo + size, :], preferred_element_type=_F32)
            st["acc"] = part if "acc" not in st else st["acc"] + part

        def final():
            h1 = st["x"] + _rms_norm(st["acc"], post1_ref[...])
            h1_ref[0, rows, :] = h1
            st["u_next"] = _rms_norm(h1, prem_ref[...]).astype(_BF16)

        def carry():
            u_scr[rows, :] = st.pop("u_next")

        return norm, gate, up, down, final, carry

    halves = (slice(0, half_rows), slice(half_rows, n_tok))
    ffn = [ffn_half(rows) for rows in halves]
    tables0, qa0, ka0, va0, qb0, kvb0 = proj_half(halves[0])
    tables1, qa1, ka1, va1, qb1, kvb1 = proj_half(halves[1])
    proj_pieces = [qa0, ka0, va0, qa1, qb0, kvb0, qb1, kvb1, ka1, va1]
    held_back = 2
    tokens = []

    def proj_slot():
        if len(proj_pieces) > held_back:
            tokens.append(proj_pieces.pop(0)())

    def take_token(lag=2):
        token = None
        while len(tokens) > lag:
            zero = tokens.pop(0)
            token = zero if token is None else token + zero
        return token

    ffn[0][0]()
    proj_slot()
    ffn[1][0]()
    proj_slot()
    tokens.append(tables0())
    proj_slot()
    tokens.append(tables1())
    last_lo = FF_CHUNKS[-1][0]
    for lo, size in FF_CHUNKS:
        for _, gate, up, _, _, _ in ffn:
            gate(lo, size)
            up(lo, size)
        for _, _, _, down, final, _ in ffn:
            down(lo, size, take_token())
            proj_slot()
            if lo == last_lo:
                final()
    while proj_pieces:
        proj_pieces.pop(0)()
    for *_, carry in ffn:
        carry()


def _bias_kernel(rel_ref, bias_ref):
    m = lax.broadcasted_iota(jnp.int32, (REL_PAD, TOEPLITZ_W), 1)
    r = lax.broadcasted_iota(jnp.int32, (REL_PAD, TOEPLITZ_W), 0)
    sel = (r == jnp.clip(PAD_A + CHUNK - 1 - m, -MAX_REL, MAX_REL) + MAX_REL).astype(_F32)
    g = jnp.dot(rel_ref[...], sel, preferred_element_type=_F32, precision=lax.Precision.HIGHEST)
    for h in range(N_HEADS_A):
        rows = jnp.broadcast_to(g[h:h + 1, :], (CHUNK, TOEPLITZ_W))
        rolled = pltpu.roll(rows, TOEPLITZ_W - (CHUNK - 1), 1, stride=1, stride_axis=0)
        bias_ref[h // 4, (h % 4) * CHUNK:(h % 4 + 1) * CHUNK, :] = rolled[:, :BAND_A] * LOG2E


def _mixer_ffn_kernel(sinks_ref, zero_ref, h1_ref, qa_ref, qb_ref, ka_ref, va_ref, kb_ref, vb_ref, bias_ref,
                      wout_ref, postm_ref, pre2_ref, wg_ref, wu_ref, wd_ref, post2_ref, final_ref,
                      out_ref, ka_all, va_all, kb_all, vb_all, o_scr):
    g = pl.program_id(0)
    n_tok = TOKENS_PER_STEP
    steps_per_seq = SEQ // n_tok
    seq_step = jnp.minimum(g, pl.num_programs(0) - 2) % steps_per_seq

    @pl.when(g == 0)
    def _():
        for ref in (o_scr, ka_all, va_all, kb_all, vb_all):
            ref[...] = jnp.zeros(ref.shape, ref.dtype)

    def dense_half(rows):
        st = {}

        def wout():
            st["mixed"] = jnp.dot(o_scr[rows, :], wout_ref[...], preferred_element_type=_F32)

        def norm():
            st["h2"] = h1_ref[0, rows, :] + _rms_norm(st.pop("mixed"), postm_ref[...])
            st["xn"] = _rms_norm(st["h2"], pre2_ref[...]).astype(_BF16)

        def gate(lo, size):
            st["g"] = jnp.dot(st["xn"], wg_ref[:, lo:lo + size], preferred_element_type=_F32)

        def up(lo, size):
            st["u"] = jnp.dot(st["xn"], wu_ref[:, lo:lo + size], preferred_element_type=_F32)

        def down(lo, size, token):
            gt = _after(st.pop("g"), token)
            a = (gt * (1.0 / (1.0 + jnp.exp2(gt * -LOG2E))) * st.pop("u")).astype(_BF16)
            part = jnp.dot(a, wd_ref[lo:lo + size, :], preferred_element_type=_F32)
            st["acc"] = part if "acc" not in st else st["acc"] + part

        def final():
            h3 = st["h2"] + _rms_norm(st["acc"], post2_ref[...])
            out_ref[0, rows, :] = _rms_norm(h3, final_ref[...])

        return wout, norm, gate, up, down, final

    half_rows = n_tok // 2
    dense = [dense_half(slice(0, half_rows)), dense_half(slice(half_rows, n_tok))]

    for hist, new, pad in ((ka_all, ka_ref, PAD_A), (va_all, va_ref, PAD_A),
                           (kb_all, kb_ref, PAD_B), (vb_all, vb_ref, PAD_B)):
        hist[0:pad] = hist[n_tok:n_tok + pad]
        hist[pad:pad + n_tok] = new[0]

    row_head = lax.broadcasted_iota(jnp.int32, (4 * CHUNK, 4 * HEAD_DIM), 0) // CHUNK
    lane_head = lax.broadcasted_iota(jnp.int32, (4 * CHUNK, 4 * HEAD_DIM), 1) // HEAD_DIM
    block_diag = row_head == lane_head
    lane_head_a = lax.broadcasted_iota(jnp.int32, (CHUNK, 4 * HEAD_DIM), 1) // HEAD_DIM
    col_a = lax.broadcasted_iota(jnp.int32, (1, BAND_A), 1)
    col_b = lax.broadcasted_iota(jnp.int32, (1, BAND_B), 1)
    low_half = lax.broadcasted_iota(jnp.int32, (CHUNK, LANES), 1) < HEAD_DIM
    row_blk = lax.broadcasted_iota(jnp.int32, (4 * CHUNK, 1), 0) // CHUNK
    sink_cols = []
    for kv in range(N_KV_B):
        col = jnp.zeros((4 * CHUNK, 1), _F32)
        for gq in range(4):
            col = jnp.where(row_blk == gq, sinks_ref[4 * kv + gq] * LOG2E, col)
        sink_cols.append(col)

    def attn_a(c, grp):
        r0 = c * CHUNK
        cols = slice(grp * 4 * HEAD_DIM, (grp + 1) * 4 * HEAD_DIM)
        st = {}

        def scores():
            inv = jnp.where(seq_step == 0, PAD_A - r0, 0)
            q4 = qa_ref[0, r0:r0 + CHUNK, cols]
            q_bd = jnp.where(block_diag, jnp.concatenate([q4] * 4, axis=0), jnp.zeros((), _BF16))
            k_band = ka_all[r0:r0 + BAND_A, cols]
            s = lax.dot_general(q_bd, k_band, _NT, preferred_element_type=_F32) + bias_ref[grp]
            s = s + jnp.where(col_a < inv, NEG_INF, 0.0)
            p = jnp.exp2(s - jnp.max(s, axis=-1, keepdims=True))
            st["p"] = (p * (1.0 / jnp.sum(p, axis=-1, keepdims=True))).astype(_BF16)

        def values():
            v_band = va_all[r0:r0 + BAND_A, cols]
            o_full = jnp.dot(st.pop("p"), v_band, preferred_element_type=_F32)
            o = jnp.zeros((CHUNK, 4 * HEAD_DIM), _F32)
            for h in range(4):
                o = jnp.where(lane_head_a == h, o_full[h * CHUNK:(h + 1) * CHUNK], o)
            o_scr[r0:r0 + CHUNK, cols] = o.astype(_BF16)
            return _zero_token(o, zero_ref[0])

        return scores, values

    def attn_b(c, kv):
        r0 = c * CHUNK
        lanes_kv = slice(kv * LANES, (kv + 1) * LANES)
        st = {}

        def scores():
            inv = jnp.where(seq_step == 0, PAD_B - r0, 0)
            k_band = kb_all[r0:r0 + BAND_B, lanes_kv]
            rows = []
            for j in (2 * kv, 2 * kv + 1):
                qc = qb_ref[0, r0:r0 + CHUNK, j * LANES:(j + 1) * LANES]
                rows.append(jnp.where(low_half, qc, jnp.zeros((), _BF16)))
                rows.append(jnp.where(low_half, jnp.zeros((), _BF16), qc))
            lhs = jnp.concatenate(rows, axis=0)
            s = lax.dot_general(lhs, k_band, _NT, preferred_element_type=_F32)
            if r0 < PAD_B:
                s = s + jnp.where(col_b < inv, NEG_INF, 0.0)
            sink = sink_cols[kv]
            m = jnp.maximum(jnp.max(s, axis=-1, keepdims=True), sink)
            e = jnp.exp2(s - m)
            denom = jnp.sum(e, axis=-1, keepdims=True) + jnp.exp2(sink - m)
            st["e"] = (e * (1.0 / denom)).astype(_BF16)

        def values():
            v_band = vb_all[r0:r0 + BAND_B, lanes_kv]
            o_full = jnp.dot(st.pop("e"), v_band, preferred_element_type=_F32)
            for jj in range(2):
                o = jnp.where(low_half, o_full[(2 * jj) * CHUNK:(2 * jj + 1) * CHUNK],
                              o_full[(2 * jj + 1) * CHUNK:(2 * jj + 2) * CHUNK])
                j = 2 * kv + jj
                o_scr[r0:r0 + CHUNK, W_A + j * LANES:W_A + (j + 1) * LANES] = o.astype(_BF16)
            return _zero_token(o, zero_ref[0])

        return scores, values

    attn = []
    for c in range(n_tok // CHUNK):
        attn += [attn_a(c, 0), attn_a(c, 1), attn_b(c, 0), attn_b(c, 1)]

    tokens = []
    started = [0]

    def attention_slot():
        j = started[0]
        if j < len(attn):
            attn[j][0]()
        if 0 <= j - 2 < len(attn):
            tokens.append(attn[j - 2][1]())
        started[0] = j + 1

    def take_token():
        token = None
        while len(tokens) > 1:
            zero = tokens.pop(0)
            token = zero if token is None else token + zero
        return token

    dense[0][0]()
    attention_slot()
    dense[1][0]()
    attention_slot()
    dense[0][1]()
    attention_slot()
    first_lo, last_lo = FF_CHUNKS[0][0], FF_CHUNKS[-1][0]
    for lo, size in FF_CHUNKS:
        for half, (_, norm, gate, up, _, _) in enumerate(dense):
            if lo == first_lo and half == 1:
                norm()
                attention_slot()
            gate(lo, size)
            attention_slot()
            up(lo, size)
            attention_slot()
        for _, _, _, _, down, final in dense:
            down(lo, size, take_token())
            attention_slot()
            if lo == last_lo:
                final()
    assert started[0] > len(attn) + 1


def _bias_call(rel):
    return pl.pallas_call(
        _bias_kernel,
        out_shape=jax.ShapeDtypeStruct((2, 4 * CHUNK, BAND_A), _F32),
        name="rel_bias_table",
    )(rel)


def _resident(shape):
    return pl.BlockSpec(shape, lambda *_: (0,) * len(shape), pipeline_mode=pl.Buffered(1))


def _ffn_proj_call(x, pos, freq, pre1, wg, wu, wd, post1, prem, win):
    batch, seq, d = x.shape
    t = TOKENS_PER_STEP
    per_seq = seq // t
    n_tiles = batch * per_seq

    def cur(w):
        def index(g):
            tile = jnp.minimum(g, n_tiles - 1)
            return (tile // per_seq, tile % per_seq, 0)
        return pl.BlockSpec((1, t, w), index)

    def prev(w):
        def index(g):
            tile = jnp.maximum(g - 1, 0)
            return (tile // per_seq, tile % per_seq, 0)
        return pl.BlockSpec((1, t, w), index)

    prev_pos = pl.BlockSpec((1, 1, t), lambda g: (jnp.maximum(g - 1, 0), 0, 0))
    run_time_zero = jnp.zeros((1,), jnp.int32)
    out_shapes = [jax.ShapeDtypeStruct((batch, seq, d), _F32)] + [
        jax.ShapeDtypeStruct((batch, seq, w), _BF16) for w in (W_A, W_A, W_A, W_QB, W_KVB_DUP, W_KVB_DUP)]
    return pl.pallas_call(
        _ffn_proj_kernel,
        grid=(n_tiles + 1,),
        in_specs=[pl.BlockSpec(memory_space=pltpu.SMEM), cur(d), prev_pos, _resident((ROT_DIM // 2, 1)),
                  _resident((1, d)),
                  _resident((d, D_FF)), _resident((d, D_FF)), _resident((D_FF, d)),
                  _resident((1, d)), _resident((1, d)), _resident((d, D_IN))],
        out_specs=[cur(d), prev(W_A), prev(W_A), prev(W_A), prev(W_QB), prev(W_KVB_DUP), prev(W_KVB_DUP)],
        out_shape=out_shapes,
        scratch_shapes=[pltpu.VMEM((t, d), _BF16)],
        compiler_params=pltpu.CompilerParams(dimension_semantics=("arbitrary",),
                                             vmem_limit_bytes=VMEM_LIMIT_BYTES),
        name="ffn1_proj",
    )(run_time_zero, x, pos, freq, pre1, wg, wu, wd, post1, prem, win)


def _mixer_ffn_call(sinks, h1, qa, ka, va, qb, kb, vb, bias, wout, postm, pre2, wg, wu, wd, post2, final):
    batch, seq, d = h1.shape
    t = TOKENS_PER_STEP
    per_seq = seq // t
    n_tiles = batch * per_seq

    def att(w):
        def index(g):
            tile = jnp.minimum(g, n_tiles - 1)
            return (tile // per_seq, tile % per_seq, 0)
        return pl.BlockSpec((1, t, w), index)

    def tail(w):
        def index(g):
            tile = jnp.maximum(g - 1, 0)
            return (tile // per_seq, tile % per_seq, 0)
        return pl.BlockSpec((1, t, w), index)

    smem = pl.BlockSpec(memory_space=pltpu.SMEM)
    run_time_zero = jnp.zeros((1,), jnp.int32)
    return pl.pallas_call(
        _mixer_ffn_kernel,
        grid=(n_tiles + 1,),
        in_specs=[smem, smem,
                  tail(d), att(W_A), att(W_QB), att(W_A), att(W_A), att(W_KVB_DUP), att(W_KVB_DUP),
                  _resident((2, 4 * CHUNK, BAND_A)),
                  _resident((d, d)), _resident((1, d)), _resident((1, d)),
                  _resident((d, D_FF)), _resident((d, D_FF)), _resident((D_FF, d)),
                  _resident((1, d)), _resident((1, d))],
        out_specs=tail(d),
        out_shape=jax.ShapeDtypeStruct((batch, seq, d), _F32),
        scratch_shapes=[pltpu.VMEM((PAD_A + t, W_A), _BF16), pltpu.VMEM((PAD_A + t, W_A), _BF16),
                        pltpu.VMEM((PAD_B + t, W_KVB_DUP), _BF16), pltpu.VMEM((PAD_B + t, W_KVB_DUP), _BF16),
                        pltpu.VMEM((t, 2 * W_A), _BF16)],
        compiler_params=pltpu.CompilerParams(dimension_semantics=("arbitrary",),
                                             vmem_limit_bytes=VMEM_LIMIT_BYTES),
        name="mixer_ffn2",
    )(sinks, run_time_zero, h1, qa, qb, ka, va, kb, vb, bias, wout, postm, pre2, wg, wu, wd, post2, final)


def kernel(x, positions, ffn1_pre_g, ffn1_w_gate, ffn1_w_up, ffn1_w_down, ffn1_post_g, mix_pre_g, w_in, rel_bias_a, sinks_b, w_out, mix_post_g, ffn2_pre_g, ffn2_w_gate, ffn2_w_up, ffn2_w_down, ffn2_post_g, final_g):
    assert x.shape[1] == SEQ and x.shape[2] == D_MODEL and ffn1_pre_g.shape[0] == 1
    row = lambda g: g[0].reshape(1, D_MODEL).astype(_F32)
    half_row = lambda g: 0.5 * row(g)
    bf = lambda w: w[0].astype(_BF16)

    half = ROT_DIM // 2
    freq = jnp.power(jnp.float32(ROPE_THETA), -jnp.arange(half, dtype=_F32) * (2.0 / ROT_DIM)).reshape(half, 1)
    pos_rows = positions.reshape(-1, 1, TOKENS_PER_STEP)

    rel_pad = jnp.pad(rel_bias_a[0].astype(_F32), ((0, 0), (0, REL_PAD - (2 * MAX_REL + 1))))

    h1, qa, ka, va, qb, kb, vb = _ffn_proj_call(
        x, pos_rows, freq, row(ffn1_pre_g), bf(ffn1_w_gate), bf(ffn1_w_up), bf(ffn1_w_down),
        half_row(ffn1_post_g), row(mix_pre_g), bf(w_in))
    return _mixer_ffn_call(
        sinks_b[0].astype(_F32), h1, qa, ka, va, qb, kb, vb, _bias_call(rel_pad), bf(w_out),
        row(mix_post_g), row(ffn2_pre_g), bf(ffn2_w_gate), bf(ffn2_w_up), bf(ffn2_w_down),
        half_row(ffn2_post_g), row(final_g))
```

```python
import jax
import jax.numpy as jnp
from jax import lax
from jax.experimental import pallas as pl
from jax.experimental.pallas import tpu as pltpu

D_MODEL = 1024
SEQ = 2048
CHUNK = 64
HEAD_DIM = 64
N_HEADS_A = 8
LEFT_CHUNKS_A = 8
MAX_REL = 128
N_HEADS_B = 8
N_KV_B = 2
LEFT_CHUNKS_B = 2
ROT_DIM = HEAD_DIM // 4
ROPE_THETA = 500000.0
D_FF = 2816
EPS = 1e-6
NEG_INF = -1e30

W_A = N_HEADS_A * HEAD_DIM
W_QB = N_HEADS_B * HEAD_DIM
W_KVB = N_KV_B * HEAD_DIM
W_KVB_DUP = 2 * W_KVB
D_IN = 3 * W_A + W_QB + 2 * W_KVB
BAND_A = (LEFT_CHUNKS_A + 1) * CHUNK
BAND_B = (LEFT_CHUNKS_B + 1) * CHUNK
PAD_A = LEFT_CHUNKS_A * CHUNK
PAD_B = LEFT_CHUNKS_B * CHUNK
SCALE = HEAD_DIM ** -0.5
LOG2E = 1.4426950408889634

LANES = 128
SUBLANES = 8
TOEPLITZ_W = 768
REL_PAD = 384

TOKENS_PER_STEP = 512
FF_CHUNKS = tuple((lo, min(512, D_FF - lo)) for lo in range(0, D_FF, 512))
VMEM_LIMIT_BYTES = 63 * 1024 * 1024

_F32 = jnp.float32
_BF16 = jnp.bfloat16
_NT = (((1,), (1,)), ((), ()))


def _rms_norm(x, g):
    return x * lax.rsqrt(jnp.mean(x * x, axis=-1, keepdims=True) + EPS) * g


def _rope_cols(x, cos, sin_lo, sin_hi):
    return x * cos + pltpu.roll(x, ROT_DIM // 2, 1) * sin_hi + pltpu.roll(x, LANES - ROT_DIM // 2, 1) * sin_lo


def _zero_token(x, zero):
    return pltpu.bitcast(pltpu.bitcast(x[0:SUBLANES, 0:LANES], jnp.int32) & zero, _F32)


def _after(x, token):
    if token is None:
        return x
    top = x[0:SUBLANES, :] + jnp.tile(token, (1, x.shape[1] // LANES))
    return jnp.concatenate([top, x[SUBLANES:, :]], axis=0)


def _ffn_proj_kernel(zero_ref, x_ref, pos_ref, freq_ref, pre1_ref, wg_ref, wu_ref, wd_ref, post1_ref, prem_ref,
                     win_ref, h1_ref, qa_ref, ka_ref, va_ref, qb_ref, kb_ref, vb_ref, u_scr):
    n_tok = TOKENS_PER_STEP
    half_rows = n_tok // 2

    @pl.when(pl.program_id(0) == 0)
    def _():
        u_scr[...] = jnp.zeros(u_scr.shape, u_scr.dtype)

    def proj_half(rows):
        st = {}

        def tables():
            n = rows.stop - rows.start
            ang = freq_ref[...] * pos_ref[0][:, rows].astype(_F32)
            packed = jnp.concatenate([jnp.cos(ang), jnp.sin(ang), jnp.zeros((LANES - ROT_DIM, n), _F32)], axis=0)
            cs = packed.T
            lane = lax.broadcasted_iota(jnp.int32, cs.shape, 1)
            in_head = lane % HEAD_DIM
            first_head = lane < HEAD_DIM
            half = ROT_DIM // 2

            def both_heads(t):
                return jnp.where(first_head, t, pltpu.roll(t, HEAD_DIM, 1))

            cos = both_heads(jnp.where(lane < half, cs, pltpu.roll(cs, half, 1)))
            sin_lo = both_heads(pltpu.roll(cs, LANES - half, 1))
            sin_hi = both_heads(cs)
            st["cos"] = jnp.where(in_head < ROT_DIM, cos, 1.0)
            st["sin_lo"] = jnp.where(in_head < half, -sin_lo, 0.0)
            st["sin_hi"] = jnp.where((in_head >= half) & (in_head < ROT_DIM), sin_hi, 0.0)
            st["low"] = first_head
            return _zero_token(st["cos"] + st["sin_hi"], zero_ref[0])

        def cols(c0, width):
            return jnp.dot(u_scr[rows, :], win_ref[:, c0:c0 + width], preferred_element_type=_F32)

        def rope(blk):
            return _rope_cols(blk, st["cos"], st["sin_lo"], st["sin_hi"])

        def twice(blk):
            swapped = pltpu.roll(blk, HEAD_DIM, 1)
            return jnp.concatenate([jnp.where(st["low"], blk, swapped), jnp.where(st["low"], swapped, blk)], axis=1)

        def qa():
            q = cols(0, W_A) * (SCALE * LOG2E)
            qa_ref[0, rows, :] = q.astype(_BF16)
            return _zero_token(q, zero_ref[0])

        def ka():
            k = cols(W_A, W_A)
            ka_ref[0, rows, :] = k.astype(_BF16)
            return _zero_token(k, zero_ref[0])

        def va():
            v = cols(2 * W_A, W_A)
            va_ref[0, rows, :] = v.astype(_BF16)
            return _zero_token(v, zero_ref[0])

        def qb():
            p = cols(3 * W_A, W_QB)
            for j in range(W_QB // LANES):
                q = rope(p[:, j * LANES:(j + 1) * LANES]) * (SCALE * LOG2E)
                qb_ref[0, rows, j * LANES:(j + 1) * LANES] = q.astype(_BF16)
            return _zero_token(q, zero_ref[0])

        def kvb():
            p = cols(3 * W_A + W_QB, 2 * W_KVB)
            k = twice(rope(p[:, 0:W_KVB]))
            kb_ref[0, rows, :] = k.astype(_BF16)
            vb_ref[0, rows, :] = twice(p[:, W_KVB:2 * W_KVB]).astype(_BF16)
            return _zero_token(k, zero_ref[0])

        return tables, qa, ka, va, qb, kvb

    def ffn_half(rows):
        st = {}

        def norm():
            st["x"] = x_ref[0, rows, :]
            st["xn"] = _rms_norm(st["x"], pre1_ref[...]).astype(_BF16)

        def gate(lo, size):
            st["g"] = jnp.dot(st["xn"], wg_ref[:, lo:lo + size], preferred_element_type=_F32)

        def up(lo, size):
            st["u"] = jnp.dot(st["xn"], wu_ref[:, lo:lo + size], preferred_element_type=_F32)

        def down(lo, size, token):
            gt = _after(st.pop("g"), token)
            a = (gt * (1.0 / (1.0 + jnp.exp2(gt * -LOG2E))) * st.pop("u")).astype(_BF16)
            part = jnp.dot(a, wd_ref[lo:lo + size, :], preferred_element_type=_F32)
            st["acc"] = part if "acc" not in st else st["acc"] + part

        def final():
            h1 = st["x"] + _rms_norm(st["acc"], post1_ref[...])
            h1_ref[0, rows, :] = h1
            st["u_next"] = _rms_norm(h1, prem_ref[...]).astype(_BF16)

        def carry():
            u_scr[rows, :] = st.pop("u_next")

        return norm, gate, up, down, final, carry

    halves = (slice(0, half_rows), slice(half_rows, n_tok))
    ffn = [ffn_half(rows) for rows in halves]
    tables0, qa0, ka0, va0, qb0, kvb0 = proj_half(halves[0])
    tables1, qa1, ka1, va1, qb1, kvb1 = proj_half(halves[1])
    proj_pieces = [qa0, ka0, va0, qa1, qb0, kvb0, qb1, kvb1, ka1, va1]
    held_back = 2
    tokens = []

    def proj_slot():
        if len(proj_pieces) > held_back:
            tokens.append(proj_pieces.pop(0)())

    def take_token(lag=2):
        token = None
        while len(tokens) > lag:
            zero = tokens.pop(0)
            token = zero if token is None else token + zero
        return token

    ffn[0][0]()
    proj_slot()
    ffn[1][0]()
    proj_slot()
    tokens.append(tables0())
    proj_slot()
    tokens.append(tables1())
    last_lo = FF_CHUNKS[-1][0]
    for lo, size in FF_CHUNKS:
        for _, gate, up, _, _, _ in ffn:
            gate(lo, size)
            up(lo, size)
        for _, _, _, down, final, _ in ffn:
            down(lo, size, take_token())
            proj_slot()
            if lo == last_lo:
                final()
    while proj_pieces:
        proj_pieces.pop(0)()
    for *_, carry in ffn:
        carry()


def _bias_kernel(rel_ref, bias_ref):
    m = lax.broadcasted_iota(jnp.int32, (REL_PAD, TOEPLITZ_W), 1)
    r = lax.broadcasted_iota(jnp.int32, (REL_PAD, TOEPLITZ_W), 0)
    sel = (r == jnp.clip(PAD_A + CHUNK - 1 - m, -MAX_REL, MAX_REL) + MAX_REL).astype(_F32)
    g = jnp.dot(rel_ref[...], sel, preferred_element_type=_F32, precision=lax.Precision.HIGHEST)
    for h in range(N_HEADS_A):
        rows = jnp.broadcast_to(g[h:h + 1, :], (CHUNK, TOEPLITZ_W))
        rolled = pltpu.roll(rows, TOEPLITZ_W - (CHUNK - 1), 1, stride=1, stride_axis=0)
        bias_ref[h // 4, (h % 4) * CHUNK:(h % 4 + 1) * CHUNK, :] = rolled[:, :BAND_A] * LOG2E


def _mixer_ffn_kernel(sinks_ref, zero_ref, h1_ref, qa_ref, qb_ref, ka_ref, va_ref, kb_ref, vb_ref, bias_ref,
                      wout_ref, postm_ref, pre2_ref, wg_ref, wu_ref, wd_ref, post2_ref, final_ref,
                      out_ref, ka_all, va_all, kb_all, vb_all, o_scr):
    g = pl.program_id(0)
    n_tok = TOKENS_PER_STEP
    steps_per_seq = SEQ // n_tok
    seq_step = jnp.minimum(g, pl.num_programs(0) - 2) % steps_per_seq

    @pl.when(g == 0)
    def _():
        for ref in (o_scr, ka_all, va_all, kb_all, vb_all):
            ref[...] = jnp.zeros(ref.shape, ref.dtype)

    def dense_half(rows):
        st = {}

        def wout():
            st["mixed"] = jnp.dot(o_scr[rows, :], wout_ref[...], preferred_element_type=_F32)

        def norm():
            st["h2"] = h1_ref[0, rows, :] + _rms_norm(st.pop("mixed"), postm_ref[...])
            st["xn"] = _rms_norm(st["h2"], pre2_ref[...]).astype(_BF16)

        def gate(lo, size):
            st["g"] = jnp.dot(st["xn"], wg_ref[:, lo:lo + size], preferred_element_type=_F32)

        def up(lo, size):
            st["u"] = jnp.dot(st["xn"], wu_ref[:, lo:lo + size], preferred_element_type=_F32)

        def down(lo, size, token):
            gt = _after(st.pop("g"), token)
            a = (gt * (1.0 / (1.0 + jnp.exp2(gt * -LOG2E))) * st.pop("u")).astype(_BF16)
            part = jnp.dot(a, wd_ref[lo:lo + size, :], preferred_element_type=_F32)
            st["acc"] = part if "acc" not in st else st["acc"] + part

        def final():
            h3 = st["h2"] + _rms_norm(st["acc"], post2_ref[...])
            out_ref[0, rows, :] = _rms_norm(h3, final_ref[...])

        return wout, norm, gate, up, down, final

    half_rows = n_tok // 2
    dense = [dense_half(slice(0, half_rows)), dense_half(slice(half_rows, n_tok))]

    for hist, new, pad in ((ka_all, ka_ref, PAD_A), (va_all, va_ref, PAD_A),
                           (kb_all, kb_ref, PAD_B), (vb_all, vb_ref, PAD_B)):
        hist[0:pad] = hist[n_tok:n_tok + pad]
        hist[pad:pad + n_tok] = new[0]

    row_head = lax.broadcasted_iota(jnp.int32, (4 * CHUNK, 4 * HEAD_DIM), 0) // CHUNK
    lane_head = lax.broadcasted_iota(jnp.int32, (4 * CHUNK, 4 * HEAD_DIM), 1) // HEAD_DIM
    block_diag = row_head == lane_head
    lane_head_a = lax.broadcasted_iota(jnp.int32, (CHUNK, 4 * HEAD_DIM), 1) // HEAD_DIM
    col_a = lax.broadcasted_iota(jnp.int32, (1, BAND_A), 1)
    col_b = lax.broadcasted_iota(jnp.int32, (1, BAND_B), 1)
    low_half = lax.broadcasted_iota(jnp.int32, (CHUNK, LANES), 1) < HEAD_DIM
    row_blk = lax.broadcasted_iota(jnp.int32, (4 * CHUNK, 1), 0) // CHUNK
    sink_cols = []
    for kv in range(N_KV_B):
        col = jnp.zeros((4 * CHUNK, 1), _F32)
        for gq in range(4):
            col = jnp.where(row_blk == gq, sinks_ref[4 * kv + gq] * LOG2E, col)
        sink_cols.append(col)

    def attn_a(c, grp):
        r0 = c * CHUNK
        cols = slice(grp * 4 * HEAD_DIM, (grp + 1) * 4 * HEAD_DIM)
        st = {}

        def scores():
            inv = jnp.where(seq_step == 0, PAD_A - r0, 0)
            q4 = qa_ref[0, r0:r0 + CHUNK, cols]
            q_bd = jnp.where(block_diag, jnp.concatenate([q4] * 4, axis=0), jnp.zeros((), _BF16))
            k_band = ka_all[r0:r0 + BAND_A, cols]
            s = lax.dot_general(q_bd, k_band, _NT, preferred_element_type=_F32) + bias_ref[grp]
            s = s + jnp.where(col_a < inv, NEG_INF, 0.0)
            p = jnp.exp2(s - jnp.max(s, axis=-1, keepdims=True))
            st["p"] = (p * (1.0 / jnp.sum(p, axis=-1, keepdims=True))).astype(_BF16)

        def values():
            v_band = va_all[r0:r0 + BAND_A, cols]
            o_full = jnp.dot(st.pop("p"), v_band, preferred_element_type=_F32)
            o = jnp.zeros((CHUNK, 4 * HEAD_DIM), _F32)
            for h in range(4):
                o = jnp.where(lane_head_a == h, o_full[h * CHUNK:(h + 1) * CHUNK], o)
            o_scr[r0:r0 + CHUNK, cols] = o.astype(_BF16)
            return _zero_token(o, zero_ref[0])

        return scores, values

    def attn_b(c, kv):
        r0 = c * CHUNK
        lanes_kv = slice(kv * LANES, (kv + 1) * LANES)
        st = {}

        def scores():
            inv = jnp.where(seq_step == 0, PAD_B - r0, 0)
            k_band = kb_all[r0:r0 + BAND_B, lanes_kv]
            rows = []
            for j in (2 * kv, 2 * kv + 1):
                qc = qb_ref[0, r0:r0 + CHUNK, j * LANES:(j + 1) * LANES]
                rows.append(jnp.where(low_half, qc, jnp.zeros((), _BF16)))
                rows.append(jnp.where(low_half, jnp.zeros((), _BF16), qc))
            lhs = jnp.concatenate(rows, axis=0)
            s = lax.dot_general(lhs, k_band, _NT, preferred_element_type=_F32)
            if r0 < PAD_B:
                s = s + jnp.where(col_b < inv, NEG_INF, 0.0)
            sink = sink_cols[kv]
            m = jnp.maximum(jnp.max(s, axis=-1, keepdims=True), sink)
            e = jnp.exp2(s - m)
            denom = jnp.sum(e, axis=-1, keepdims=True) + jnp.exp2(sink - m)
            st["e"] = (e * (1.0 / denom)).astype(_BF16)

        def values():
            v_band = vb_all[r0:r0 + BAND_B, lanes_kv]
            o_full = jnp.dot(st.pop("e"), v_band, preferred_element_type=_F32)
            for jj in range(2):
                o = jnp.where(low_half, o_full[(2 * jj) * CHUNK:(2 * jj + 1) * CHUNK],
                              o_full[(2 * jj + 1) * CHUNK:(2 * jj + 2) * CHUNK])
                j = 2 * kv + jj
                o_scr[r0:r0 + CHUNK, W_A + j * LANES:W_A + (j + 1) * LANES] = o.astype(_BF16)
            return _zero_token(o, zero_ref[0])

        return scores, values

    attn = []
    for c in range(n_tok // CHUNK):
        attn += [attn_a(c, 0), attn_b(c, 0), attn_a(c, 1), attn_b(c, 1)]

    tokens = []
    started = [0]

    def attention_slot():
        j = started[0]
        if j < len(attn):
            attn[j][0]()
        if 0 <= j - 2 < len(attn):
            tokens.append(attn[j - 2][1]())
        started[0] = j + 1

    def take_token():
        token = None
        while len(tokens) > 2:
            zero = tokens.pop(0)
            token = zero if token is None else token + zero
        return token

    dense[0][0]()
    attention_slot()
    dense[1][0]()
    attention_slot()
    dense[0][1]()
    attention_slot()
    first_lo, last_lo = FF_CHUNKS[0][0], FF_CHUNKS[-1][0]
    for lo, size in FF_CHUNKS:
        for half, (_, norm, gate, up, _, _) in enumerate(dense):
            if lo == first_lo and half == 1:
                norm()
                attention_slot()
            gate(lo, size)
            attention_slot()
            up(lo, size)
            attention_slot()
        for _, _, _, _, down, final in dense:
            down(lo, size, take_token())
            attention_slot()
            if lo == last_lo:
                final()
    assert started[0] > len(attn) + 1


def _bias_call(rel):
    return pl.pallas_call(
        _bias_kernel,
        out_shape=jax.ShapeDtypeStruct((2, 4 * CHUNK, BAND_A), _F32),
        name="rel_bias_table",
    )(rel)


def _resident(shape):
    return pl.BlockSpec(shape, lambda *_: (0,) * len(shape), pipeline_mode=pl.Buffered(1))


def _ffn_proj_call(x, pos, freq, pre1, wg, wu, wd, post1, prem, win):
    batch, seq, d = x.shape
    t = TOKENS_PER_STEP
    per_seq = seq // t
    n_tiles = batch * per_seq

    def cur(w):
        def index(g):
            tile = jnp.minimum(g, n_tiles - 1)
            return (tile // per_seq, tile % per_seq, 0)
        return pl.BlockSpec((1, t, w), index)

    def prev(w):
        def index(g):
            tile = jnp.maximum(g - 1, 0)
            return (tile // per_seq, tile % per_seq, 0)
        return pl.BlockSpec((1, t, w), index)

    prev_pos = pl.BlockSpec((1, 1, t), lambda g: (jnp.maximum(g - 1, 0), 0, 0))
    run_time_zero = jnp.zeros((1,), jnp.int32)
    out_shapes = [jax.ShapeDtypeStruct((batch, seq, d), _F32)] + [
        jax.ShapeDtypeStruct((batch, seq, w), _BF16) for w in (W_A, W_A, W_A, W_QB, W_KVB_DUP, W_KVB_DUP)]
    return pl.pallas_call(
        _ffn_proj_kernel,
        grid=(n_tiles + 1,),
        in_specs=[pl.BlockSpec(memory_space=pltpu.SMEM), cur(d), prev_pos, _resident((ROT_DIM // 2, 1)),
                  _resident((1, d)),
                  _resident((d, D_FF)), _resident((d, D_FF)), _resident((D_FF, d)),
                  _resident((1, d)), _resident((1, d)), _resident((d, D_IN))],
        out_specs=[cur(d), prev(W_A), prev(W_A), prev(W_A), prev(W_QB), prev(W_KVB_DUP), prev(W_KVB_DUP)],
        out_shape=out_shapes,
        scratch_shapes=[pltpu.VMEM((t, d), _BF16)],
        compiler_params=pltpu.CompilerParams(dimension_semantics=("arbitrary",),
                                             vmem_limit_bytes=VMEM_LIMIT_BYTES),
        name="ffn1_proj",
    )(run_time_zero, x, pos, freq, pre1, wg, wu, wd, post1, prem, win)


def _mixer_ffn_call(sinks, h1, qa, ka, va, qb, kb, vb, bias, wout, postm, pre2, wg, wu, wd, post2, final):
    batch, seq, d = h1.shape
    t = TOKENS_PER_STEP
    per_seq = seq // t
    n_tiles = batch * per_seq

    def att(w):
        def index(g):
            tile = jnp.minimum(g, n_tiles - 1)
            return (tile // per_seq, tile % per_seq, 0)
        return pl.BlockSpec((1, t, w), index)

    def tail(w):
        def index(g):
            tile = jnp.maximum(g - 1, 0)
            return (tile // per_seq, tile % per_seq, 0)
        return pl.BlockSpec((1, t, w), index)

    smem = pl.BlockSpec(memory_space=pltpu.SMEM)
    run_time_zero = jnp.zeros((1,), jnp.int32)
    return pl.pallas_call(
        _mixer_ffn_kernel,
        grid=(n_tiles + 1,),
        in_specs=[smem, smem,
                  tail(d), att(W_A), att(W_QB), att(W_A), att(W_A), att(W_KVB_DUP), att(W_KVB_DUP),
                  _resident((2, 4 * CHUNK, BAND_A)),
                  _resident((d, d)), _resident((1, d)), _resident((1, d)),
                  _resident((d, D_FF)), _resident((d, D_FF)), _resident((D_FF, d)),
                  _resident((1, d)), _resident((1, d))],
        out_specs=tail(d),
        out_shape=jax.ShapeDtypeStruct((batch, seq, d), _F32),
        scratch_shapes=[pltpu.VMEM((PAD_A + t, W_A), _BF16), pltpu.VMEM((PAD_A + t, W_A), _BF16),
                        pltpu.VMEM((PAD_B + t, W_KVB_DUP), _BF16), pltpu.VMEM((PAD_B + t, W_KVB_DUP), _BF16),
                        pltpu.VMEM((t, 2 * W_A), _BF16)],
        compiler_params=pltpu.CompilerParams(dimension_semantics=("arbitrary",),
                                             vmem_limit_bytes=VMEM_LIMIT_BYTES),
        name="mixer_ffn2",
    )(sinks, run_time_zero, h1, qa, qb, ka, va, kb, vb, bias, wout, postm, pre2, wg, wu, wd, post2, final)


def kernel(x, positions, ffn1_pre_g, ffn1_w_gate, ffn1_w_up, ffn1_w_down, ffn1_post_g, mix_pre_g, w_in, rel_bias_a, sinks_b, w_out, mix_post_g, ffn2_pre_g, ffn2_w_gate, ffn2_w_up, ffn2_w_down, ffn2_post_g, final_g):
    assert x.shape[1] == SEQ and x.shape[2] == D_MODEL and ffn1_pre_g.shape[0] == 1
    row = lambda g: g[0].reshape(1, D_MODEL).astype(_F32)
    half_row = lambda g: 0.5 * row(g)
    bf = lambda w: w[0].astype(_BF16)

    half = ROT_DIM // 2
    freq = jnp.power(jnp.float32(ROPE_THETA), -jnp.arange(half, dtype=_F32) * (2.0 / ROT_DIM)).reshape(half, 1)
    pos_rows = positions.reshape(-1, 1, TOKENS_PER_STEP)

    rel_pad = jnp.pad(rel_bias_a[0].astype(_F32), ((0, 0), (0, REL_PAD - (2 * MAX_REL + 1))))

    h1, qa, ka, va, qb, kb, vb = _ffn_proj_call(
        x, pos_rows, freq, row(ffn1_pre_g), bf(ffn1_w_gate), bf(ffn1_w_up), bf(ffn1_w_down),
        half_row(ffn1_post_g), row(mix_pre_g), bf(w_in))
    return _mixer_ffn_call(
        sinks_b[0].astype(_F32), h1, qa, ka, va, qb, kb, vb, _bias_call(rel_pad), bf(w_out),
        row(mix_post_g), row(ffn2_pre_g), bf(ffn2_w_gate), bf(ffn2_w_up), bf(ffn2_w_down),
        half_row(ffn2_post_g), row(final_g))
```

```python
import jax
import jax.numpy as jnp
from jax import lax
from jax.experimental import pallas as pl
from jax.experimental.pallas import tpu as pltpu

D_MODEL = 1024
SEQ = 2048
CHUNK = 64
HEAD_DIM = 64
N_HEADS_A = 8
LEFT_CHUNKS_A = 8
MAX_REL = 128
N_HEADS_B = 8
N_KV_B = 2
LEFT_CHUNKS_B = 2
ROT_DIM = HEAD_DIM // 4
ROPE_THETA = 500000.0
D_FF = 2816
EPS = 1e-6
NEG_INF = -1e30

W_A = N_HEADS_A * HEAD_DIM
W_QB = N_HEADS_B * HEAD_DIM
W_KVB = N_KV_B * HEAD_DIM
W_KVB_DUP = 2 * W_KVB
D_IN = 3 * W_A + W_QB + 2 * W_KVB
BAND_A = (LEFT_CHUNKS_A + 1) * CHUNK
BAND_B = (LEFT_CHUNKS_B + 1) * CHUNK
PAD_A = LEFT_CHUNKS_A * CHUNK
PAD_B = LEFT_CHUNKS_B * CHUNK
SCALE = HEAD_DIM ** -0.5
LOG2E = 1.4426950408889634

LANES = 128
SUBLANES = 8
TOEPLITZ_W = 768
REL_PAD = 384

TOKENS_PER_STEP = 512
FF_CHUNKS = tuple((lo, min(512, D_FF - lo)) for lo in range(0, D_FF, 512))
VMEM_LIMIT_BYTES = 63 * 1024 * 1024

_F32 = jnp.float32
_BF16 = jnp.bfloat16
_NT = (((1,), (1,)), ((), ()))


def _rms_norm(x, g):
    return x * lax.rsqrt(jnp.mean(x * x, axis=-1, keepdims=True) + EPS) * g


def _rope_cols(x, cos, sin_lo, sin_hi):
    return x * cos + pltpu.roll(x, ROT_DIM // 2, 1) * sin_hi + pltpu.roll(x, LANES - ROT_DIM // 2, 1) * sin_lo


def _zero_token(x, zero):
    return pltpu.bitcast(pltpu.bitcast(x[0:SUBLANES, 0:LANES], jnp.int32) & zero, _F32)


def _after(x, token):
    if token is None:
        return x
    top = x[0:SUBLANES, :] + jnp.tile(token, (1, x.shape[1] // LANES))
    return jnp.concatenate([top, x[SUBLANES:, :]], axis=0)


def _ffn_proj_kernel(zero_ref, x_ref, pos_ref, freq_ref, pre1_ref, wg_ref, wu_ref, wd_ref, post1_ref, prem_ref,
                     win_ref, h1_ref, qa_ref, ka_ref, va_ref, qb_ref, kb_ref, vb_ref, u_scr):
    n_tok = TOKENS_PER_STEP
    half_rows = n_tok // 2

    @pl.when(pl.program_id(0) == 0)
    def _():
        u_scr[...] = jnp.zeros(u_scr.shape, u_scr.dtype)

    def proj_half(rows):
        st = {}

        def tables():
            n = rows.stop - rows.start
            ang = freq_ref[...] * pos_ref[0][:, rows].astype(_F32)
            packed = jnp.concatenate([jnp.cos(ang), jnp.sin(ang), jnp.zeros((LANES - ROT_DIM, n), _F32)], axis=0)
            cs = packed.T
            lane = lax.broadcasted_iota(jnp.int32, cs.shape, 1)
            in_head = lane % HEAD_DIM
            first_head = lane < HEAD_DIM
            half = ROT_DIM // 2

            def both_heads(t):
                return jnp.where(first_head, t, pltpu.roll(t, HEAD_DIM, 1))

            cos = both_heads(jnp.where(lane < half, cs, pltpu.roll(cs, half, 1)))
            sin_lo = both_heads(pltpu.roll(cs, LANES - half, 1))
            sin_hi = both_heads(cs)
            st["cos"] = jnp.where(in_head < ROT_DIM, cos, 1.0)
            st["sin_lo"] = jnp.where(in_head < half, -sin_lo, 0.0)
            st["sin_hi"] = jnp.where((in_head >= half) & (in_head < ROT_DIM), sin_hi, 0.0)
            st["low"] = first_head
            return _zero_token(st["cos"] + st["sin_hi"], zero_ref[0])

        def cols(c0, width):
            return jnp.dot(u_scr[rows, :], win_ref[:, c0:c0 + width], preferred_element_type=_F32)

        def rope(blk):
            return _rope_cols(blk, st["cos"], st["sin_lo"], st["sin_hi"])

        def twice(blk):
            swapped = pltpu.roll(blk, HEAD_DIM, 1)
            return jnp.concatenate([jnp.where(st["low"], blk, swapped), jnp.where(st["low"], swapped, blk)], axis=1)

        def qa():
            q = cols(0, W_A) * (SCALE * LOG2E)
            qa_ref[0, rows, :] = q.astype(_BF16)
            return _zero_token(q, zero_ref[0])

        def ka():
            k = cols(W_A, W_A)
            ka_ref[0, rows, :] = k.astype(_BF16)
            return _zero_token(k, zero_ref[0])

        def va():
            v = cols(2 * W_A, W_A)
            va_ref[0, rows, :] = v.astype(_BF16)
            return _zero_token(v, zero_ref[0])

        def qb():
            p = cols(3 * W_A, W_QB)
            for j in range(W_QB // LANES):
                q = rope(p[:, j * LANES:(j + 1) * LANES]) * (SCALE * LOG2E)
                qb_ref[0, rows, j * LANES:(j + 1) * LANES] = q.astype(_BF16)
            return _zero_token(q, zero_ref[0])

        def kvb():
            p = cols(3 * W_A + W_QB, 2 * W_KVB)
            k = twice(rope(p[:, 0:W_KVB]))
            kb_ref[0, rows, :] = k.astype(_BF16)
            vb_ref[0, rows, :] = twice(p[:, W_KVB:2 * W_KVB]).astype(_BF16)
            return _zero_token(k, zero_ref[0])

        return tables, qa, ka, va, qb, kvb

    def ffn_half(rows):
        st = {}

        def norm():
            st["x"] = x_ref[0, rows, :]
            st["xn"] = _rms_norm(st["x"], pre1_ref[...]).astype(_BF16)

        def gate(lo, size):
            st["g"] = jnp.dot(st["xn"], wg_ref[:, lo:lo + size], preferred_element_type=_F32)

        def up(lo, size):
            st["u"] = jnp.dot(st["xn"], wu_ref[:, lo:lo + size], preferred_element_type=_F32)

        def down(lo, size, token):
            gt = _after(st.pop("g"), token)
            a = (gt * (1.0 / (1.0 + jnp.exp2(gt * -LOG2E))) * st.pop("u")).astype(_BF16)
            part = jnp.dot(a, wd_ref[lo:lo + size, :], preferred_element_type=_F32)
            st["acc"] = part if "acc" not in st else st["acc"] + part

        def final():
            h1 = st["x"] + _rms_norm(st["acc"], post1_ref[...])
            h1_ref[0, rows, :] = h1
            st["u_next"] = _rms_norm(h1, prem_ref[...]).astype(_BF16)

        def carry():
            u_scr[rows, :] = st.pop("u_next")

        return norm, gate, up, down, final, carry

    halves = (slice(0, half_rows), slice(half_rows, n_tok))
    ffn = [ffn_half(rows) for rows in halves]
    tables0, qa0, ka0, va0, qb0, kvb0 = proj_half(halves[0])
    tables1, qa1, ka1, va1, qb1, kvb1 = proj_half(halves[1])
    proj_pieces = [qa0, ka0, va0, qa1, qb0, kvb0, qb1, kvb1, ka1, va1]
    held_back = 2
    tokens = []

    def proj_slot():
        if len(proj_pieces) > held_back:
            tokens.append(proj_pieces.pop(0)())

    def take_token(lag=2):
        token = None
        while len(tokens) > lag:
            zero = tokens.pop(0)
            token = zero if token is None else token + zero
        return token

    ffn[0][0]()
    proj_slot()
    ffn[1][0]()
    proj_slot()
    tokens.append(tables0())
    proj_slot()
    tokens.append(tables1())
    last_lo = FF_CHUNKS[-1][0]
    for lo, size in FF_CHUNKS:
        for _, gate, up, _, _, _ in ffn:
            gate(lo, size)
            up(lo, size)
        for _, _, _, down, final, _ in ffn:
            down(lo, size, take_token())
            proj_slot()
            if lo == last_lo:
                final()
    while proj_pieces:
        proj_pieces.pop(0)()
    for *_, carry in ffn:
        carry()


def _bias_kernel(rel_ref, bias_ref):
    m = lax.broadcasted_iota(jnp.int32, (REL_PAD, TOEPLITZ_W), 1)
    r = lax.broadcasted_iota(jnp.int32, (REL_PAD, TOEPLITZ_W), 0)
    sel = (r == jnp.clip(PAD_A + CHUNK - 1 - m, -MAX_REL, MAX_REL) + MAX_REL).astype(_F32)
    g = jnp.dot(rel_ref[...], sel, preferred_element_type=_F32, precision=lax.Precision.HIGHEST)
    for h in range(N_HEADS_A):
        rows = jnp.broadcast_to(g[h:h + 1, :], (CHUNK, TOEPLITZ_W))
        rolled = pltpu.roll(rows, TOEPLITZ_W - (CHUNK - 1), 1, stride=1, stride_axis=0)
        bias_ref[h // 4, (h % 4) * CHUNK:(h % 4 + 1) * CHUNK, :] = rolled[:, :BAND_A] * LOG2E


def _mixer_ffn_kernel(sinks_ref, zero_ref, h1_ref, qa_ref, qb_ref, ka_ref, va_ref, kb_ref, vb_ref, bias_ref,
                      wout_ref, postm_ref, pre2_ref, wg_ref, wu_ref, wd_ref, post2_ref, final_ref,
                      out_ref, ka_all, va_all, kb_all, vb_all, o_scr):
    g = pl.program_id(0)
    n_tok = TOKENS_PER_STEP
    steps_per_seq = SEQ // n_tok
    seq_step = jnp.minimum(g, pl.num_programs(0) - 2) % steps_per_seq

    @pl.when(g == 0)
    def _():
        for ref in (o_scr, ka_all, va_all, kb_all, vb_all):
            ref[...] = jnp.zeros(ref.shape, ref.dtype)

    def dense_half(rows):
        st = {}

        def wout():
            st["mixed"] = jnp.dot(o_scr[rows, :], wout_ref[...], preferred_element_type=_F32)

        def norm():
            st["h2"] = h1_ref[0, rows, :] + _rms_norm(st.pop("mixed"), postm_ref[...])
            st["xn"] = _rms_norm(st["h2"], pre2_ref[...]).astype(_BF16)

        def gate(lo, size):
            st["g"] = jnp.dot(st["xn"], wg_ref[:, lo:lo + size], preferred_element_type=_F32)

        def up(lo, size):
            st["u"] = jnp.dot(st["xn"], wu_ref[:, lo:lo + size], preferred_element_type=_F32)

        def down(lo, size, token):
            gt = _after(st.pop("g"), token)
            a = (gt * (1.0 / (1.0 + jnp.exp2(gt * -LOG2E))) * st.pop("u")).astype(_BF16)
            part = jnp.dot(a, wd_ref[lo:lo + size, :], preferred_element_type=_F32)
            st["acc"] = part if "acc" not in st else st["acc"] + part

        def final():
            h3 = st["h2"] + _rms_norm(st["acc"], post2_ref[...])
            out_ref[0, rows, :] = _rms_norm(h3, final_ref[...])

        return wout, norm, gate, up, down, final

    bounds = (0, n_tok // 2, 3 * n_tok // 4, n_tok)
    dense = [dense_half(slice(lo, hi)) for lo, hi in zip(bounds[:-1], bounds[1:])]

    for hist, new, pad in ((ka_all, ka_ref, PAD_A), (va_all, va_ref, PAD_A),
                           (kb_all, kb_ref, PAD_B), (vb_all, vb_ref, PAD_B)):
        hist[0:pad] = hist[n_tok:n_tok + pad]
        hist[pad:pad + n_tok] = new[0]

    row_head = lax.broadcasted_iota(jnp.int32, (4 * CHUNK, 4 * HEAD_DIM), 0) // CHUNK
    lane_head = lax.broadcasted_iota(jnp.int32, (4 * CHUNK, 4 * HEAD_DIM), 1) // HEAD_DIM
    block_diag = row_head == lane_head
    lane_head_a = lax.broadcasted_iota(jnp.int32, (CHUNK, 4 * HEAD_DIM), 1) // HEAD_DIM
    col_a = lax.broadcasted_iota(jnp.int32, (1, BAND_A), 1)
    col_b = lax.broadcasted_iota(jnp.int32, (1, BAND_B), 1)
    low_half = lax.broadcasted_iota(jnp.int32, (CHUNK, LANES), 1) < HEAD_DIM
    row_blk = lax.broadcasted_iota(jnp.int32, (4 * CHUNK, 1), 0) // CHUNK
    sink_cols = []
    for kv in range(N_KV_B):
        col = jnp.zeros((4 * CHUNK, 1), _F32)
        for gq in range(4):
            col = jnp.where(row_blk == gq, sinks_ref[4 * kv + gq] * LOG2E, col)
        sink_cols.append(col)

    def attn_a(c, grp):
        r0 = c * CHUNK
        cols = slice(grp * 4 * HEAD_DIM, (grp + 1) * 4 * HEAD_DIM)
        st = {}

        def scores():
            inv = jnp.where(seq_step == 0, PAD_A - r0, 0)
            q4 = qa_ref[0, r0:r0 + CHUNK, cols]
            q_bd = jnp.where(block_diag, jnp.concatenate([q4] * 4, axis=0), jnp.zeros((), _BF16))
            k_band = ka_all[r0:r0 + BAND_A, cols]
            s = lax.dot_general(q_bd, k_band, _NT, preferred_element_type=_F32) + bias_ref[grp]
            s = s + jnp.where(col_a < inv, NEG_INF, 0.0)
            p = jnp.exp2(s - jnp.max(s, axis=-1, keepdims=True))
            st["p"] = (p * (1.0 / jnp.sum(p, axis=-1, keepdims=True))).astype(_BF16)

        def values():
            v_band = va_all[r0:r0 + BAND_A, cols]
            o_full = jnp.dot(st.pop("p"), v_band, preferred_element_type=_F32)
            o = jnp.zeros((CHUNK, 4 * HEAD_DIM), _F32)
            for h in range(4):
                o = jnp.where(lane_head_a == h, o_full[h * CHUNK:(h + 1) * CHUNK], o)
            o_scr[r0:r0 + CHUNK, cols] = o.astype(_BF16)
            return _zero_token(o, zero_ref[0])

        return scores, values

    def attn_b(c, kv):
        r0 = c * CHUNK
        lanes_kv = slice(kv * LANES, (kv + 1) * LANES)
        st = {}

        def scores():
            inv = jnp.where(seq_step == 0, PAD_B - r0, 0)
            k_band = kb_all[r0:r0 + BAND_B, lanes_kv]
            rows = []
            for j in (2 * kv, 2 * kv + 1):
                qc = qb_ref[0, r0:r0 + CHUNK, j * LANES:(j + 1) * LANES]
                rows.append(jnp.where(low_half, qc, jnp.zeros((), _BF16)))
                rows.append(jnp.where(low_half, jnp.zeros((), _BF16), qc))
            lhs = jnp.concatenate(rows, axis=0)
            s = lax.dot_general(lhs, k_band, _NT, preferred_element_type=_F32)
            if r0 < PAD_B:
                s = s + jnp.where(col_b < inv, NEG_INF, 0.0)
            sink = sink_cols[kv]
            m = jnp.maximum(jnp.max(s, axis=-1, keepdims=True), sink)
            e = jnp.exp2(s - m)
            denom = jnp.sum(e, axis=-1, keepdims=True) + jnp.exp2(sink - m)
            st["e"] = (e * (1.0 / denom)).astype(_BF16)

        def values():
            v_band = vb_all[r0:r0 + BAND_B, lanes_kv]
            o_full = jnp.dot(st.pop("e"), v_band, preferred_element_type=_F32)
            for jj in range(2):
                o = jnp.where(low_half, o_full[(2 * jj) * CHUNK:(2 * jj + 1) * CHUNK],
                              o_full[(2 * jj + 1) * CHUNK:(2 * jj + 2) * CHUNK])
                j = 2 * kv + jj
                o_scr[r0:r0 + CHUNK, W_A + j * LANES:W_A + (j + 1) * LANES] = o.astype(_BF16)
            return _zero_token(o, zero_ref[0])

        return scores, values

    attn = []
    for c in range(n_tok // CHUNK):
        attn += [attn_a(c, 0), attn_b(c, 0), attn_a(c, 1), attn_b(c, 1)]

    tokens = []
    started = [0]

    def attention_slot():
        j = started[0]
        if j < len(attn):
            attn[j][0]()
        if 0 <= j - 2 < len(attn):
            tokens.append(attn[j - 2][1]())
        started[0] = j + 1

    def take_token():
        token = None
        while len(tokens) > 2:
            zero = tokens.pop(0)
            token = zero if token is None else token + zero
        return token

    dense[0][0]()
    attention_slot()
    dense[1][0]()
    attention_slot()
    dense[2][0]()
    attention_slot()
    dense[0][1]()
    attention_slot()
    first_lo, last_lo = FF_CHUNKS[0][0], FF_CHUNKS[-1][0]
    for lo, size in FF_CHUNKS:
        for half, (_, norm, gate, up, _, _) in enumerate(dense):
            if lo == first_lo and half >= 1:
                norm()
                attention_slot()
            gate(lo, size)
            attention_slot()
            up(lo, size)
            attention_slot()
        for _, _, _, _, down, final in dense:
            down(lo, size, take_token())
            attention_slot()
            if lo == last_lo:
                final()
    assert started[0] > len(attn) + 1


def _bias_call(rel):
    return pl.pallas_call(
        _bias_kernel,
        out_shape=jax.ShapeDtypeStruct((2, 4 * CHUNK, BAND_A), _F32),
        name="rel_bias_table",
    )(rel)


def _resident(shape):
    return pl.BlockSpec(shape, lambda *_: (0,) * len(shape), pipeline_mode=pl.Buffered(1))


def _ffn_proj_call(x, pos, freq, pre1, wg, wu, wd, post1, prem, win):
    batch, seq, d = x.shape
    t = TOKENS_PER_STEP
    per_seq = seq // t
    n_tiles = batch * per_seq

    def cur(w):
        def index(g):
            tile = jnp.minimum(g, n_tiles - 1)
            return (tile // per_seq, tile % per_seq, 0)
        return pl.BlockSpec((1, t, w), index)

    def prev(w):
        def index(g):
            tile = jnp.maximum(g - 1, 0)
            return (tile // per_seq, tile % per_seq, 0)
        return pl.BlockSpec((1, t, w), index)

    prev_pos = pl.BlockSpec((1, 1, t), lambda g: (jnp.maximum(g - 1, 0), 0, 0))
    run_time_zero = jnp.zeros((1,), jnp.int32)
    out_shapes = [jax.ShapeDtypeStruct((batch, seq, d), _F32)] + [
        jax.ShapeDtypeStruct((batch, seq, w), _BF16) for w in (W_A, W_A, W_A, W_QB, W_KVB_DUP, W_KVB_DUP)]
    return pl.pallas_call(
        _ffn_proj_kernel,
        grid=(n_tiles + 1,),
        in_specs=[pl.BlockSpec(memory_space=pltpu.SMEM), cur(d), prev_pos, _resident((ROT_DIM // 2, 1)),
                  _resident((1, d)),
                  _resident((d, D_FF)), _resident((d, D_FF)), _resident((D_FF, d)),
                  _resident((1, d)), _resident((1, d)), _resident((d, D_IN))],
        out_specs=[cur(d), prev(W_A), prev(W_A), prev(W_A), prev(W_QB), prev(W_KVB_DUP), prev(W_KVB_DUP)],
        out_shape=out_shapes,
        scratch_shapes=[pltpu.VMEM((t, d), _BF16)],
        compiler_params=pltpu.CompilerParams(dimension_semantics=("arbitrary",),
                                             vmem_limit_bytes=VMEM_LIMIT_BYTES),
        name="ffn1_proj",
    )(run_time_zero, x, pos, freq, pre1, wg, wu, wd, post1, prem, win)


def _mixer_ffn_call(sinks, h1, qa, ka, va, qb, kb, vb, bias, wout, postm, pre2, wg, wu, wd, post2, final):
    batch, seq, d = h1.shape
    t = TOKENS_PER_STEP
    per_seq = seq // t
    n_tiles = batch * per_seq

    def att(w):
        def index(g):
            tile = jnp.minimum(g, n_tiles - 1)
            return (tile // per_seq, tile % per_seq, 0)
        return pl.BlockSpec((1, t, w), index)

    def tail(w):
        def index(g):
            tile = jnp.maximum(g - 1, 0)
            return (tile // per_seq, tile % per_seq, 0)
        return pl.BlockSpec((1, t, w), index)

    smem = pl.BlockSpec(memory_space=pltpu.SMEM)
    run_time_zero = jnp.zeros((1,), jnp.int32)
    return pl.pallas_call(
        _mixer_ffn_kernel,
        grid=(n_tiles + 1,),
        in_specs=[smem, smem,
                  tail(d), att(W_A), att(W_QB), att(W_A), att(W_A), att(W_KVB_DUP), att(W_KVB_DUP),
                  _resident((2, 4 * CHUNK, BAND_A)),
                  _resident((d, d)), _resident((1, d)), _resident((1, d)),
                  _resident((d, D_FF)), _resident((d, D_FF)), _resident((D_FF, d)),
                  _resident((1, d)), _resident((1, d))],
        out_specs=tail(d),
        out_shape=jax.ShapeDtypeStruct((batch, seq, d), _F32),
        scratch_shapes=[pltpu.VMEM((PAD_A + t, W_A), _BF16), pltpu.VMEM((PAD_A + t, W_A), _BF16),
                        pltpu.VMEM((PAD_B + t, W_KVB_DUP), _BF16), pltpu.VMEM((PAD_B + t, W_KVB_DUP), _BF16),
                        pltpu.VMEM((t, 2 * W_A), _BF16)],
        compiler_params=pltpu.CompilerParams(dimension_semantics=("arbitrary",),
                                             vmem_limit_bytes=VMEM_LIMIT_BYTES),
        name="mixer_ffn2",
    )(sinks, run_time_zero, h1, qa, qb, ka, va, kb, vb, bias, wout, postm, pre2, wg, wu, wd, post2, final)


def kernel(x, positions, ffn1_pre_g, ffn1_w_gate, ffn1_w_up, ffn1_w_down, ffn1_post_g, mix_pre_g, w_in, rel_bias_a, sinks_b, w_out, mix_post_g, ffn2_pre_g, ffn2_w_gate, ffn2_w_up, ffn2_w_down, ffn2_post_g, final_g):
    assert x.shape[1] == SEQ and x.shape[2] == D_MODEL and ffn1_pre_g.shape[0] == 1
    row = lambda g: g[0].reshape(1, D_MODEL).astype(_F32)
    half_row = lambda g: 0.5 * row(g)
    bf = lambda w: w[0].astype(_BF16)

    half = ROT_DIM // 2
    freq = jnp.power(jnp.float32(ROPE_THETA), -jnp.arange(half, dtype=_F32) * (2.0 / ROT_DIM)).reshape(half, 1)
    pos_rows = positions.reshape(-1, 1, TOKENS_PER_STEP)

    rel_pad = jnp.pad(rel_bias_a[0].astype(_F32), ((0, 0), (0, REL_PAD - (2 * MAX_REL + 1))))

    h1, qa, ka, va, qb, kb, vb = _ffn_proj_call(
        x, pos_rows, freq, row(ffn1_pre_g), bf(ffn1_w_gate), bf(ffn1_w_up), bf(ffn1_w_down),
        half_row(ffn1_post_g), row(mix_pre_g), bf(w_in))
    return _mixer_ffn_call(
        sinks_b[0].astype(_F32), h1, qa, ka, va, qb, kb, vb, _bias_call(rel_pad), bf(w_out),
        row(mix_post_g), row(ffn2_pre_g), bf(ffn2_w_gate), bf(ffn2_w_up), bf(ffn2_w_down),
        half_row(ffn2_post_g), row(final_g))
```
